```python
import math
import jax, jax.numpy as jnp
from jax import lax
import numpy as np

D_MODEL = 1024
BATCH = 16
SEQ = 4096
DEPTH = 4

CHUNK = 64
N_MIXERS = 3
N_CONV_LAYERS = (DEPTH + 2) // 3
N_DIFF_LAYERS = (DEPTH + 1) // 3
N_POOL_LAYERS = DEPTH // 3
D_FF = -(-8 * D_MODEL // (3 * 256)) * 256
CONV_WIDTH = 3
DIFF_HEADS = 8
DIFF_HEAD_DIM = D_MODEL // (2 * DIFF_HEADS)
Q_BLOCK = 128
POOL_WINDOWS = (2, 4, 8, 16)
POOL_GROUPS = len(POOL_WINDOWS)
POOL_GROUP_DIM = D_MODEL // POOL_GROUPS
N_MOD = 6
EPS = 1e-6

kernel_name = 'hybrid_conv_diffattn_pool_stream_encoder'


def _rms(x, g):
    xf = x.astype(jnp.float32)
    y = xf * lax.rsqrt(jnp.mean(xf * xf, axis=-1, keepdims=True) + EPS)
    return (y * g.astype(jnp.float32)).astype(x.dtype)


def _alibi_slopes(n):
    return 2.0 ** (-8.0 * jnp.arange(1, n + 1, dtype=jnp.float32) / n)


def _conv_mixer(h, w_in, conv_w, w_out):
    z = h @ w_in
    bg, cg, xv = jnp.split(z, 3, axis=-1)
    u = cg * xv
    y = lax.conv_general_dilated(
        u, conv_w[:, None, :], window_strides=(1,), padding=[(CONV_WIDTH - 1, 0)],
        dimension_numbers=('NWC', 'WIO', 'NWC'), feature_group_count=D_MODEL)
    return (bg * y) @ w_out


def _diff_attention(h, w_qkv, q_norm, k_norm, lq1, lk1, lq2, lk2, subln, w_out, lam_init):
    b, s, _ = h.shape
    hd = DIFF_HEAD_DIM
    q, k, v = jnp.split(h @ w_qkv, 3, axis=-1)
    q = _rms(q.reshape(b, s, DIFF_HEADS, 2, hd), q_norm) * (hd ** -0.5)
    k = _rms(k.reshape(b, s, DIFF_HEADS, 2, hd), k_norm)
    v = v.reshape(b, s, DIFF_HEADS, 2 * hd)
    q = jnp.transpose(q, (0, 2, 3, 1, 4))
    k = jnp.transpose(k, (0, 2, 3, 1, 4))
    v = jnp.transpose(v, (0, 2, 1, 3))
    lam = (jnp.exp(jnp.sum(lq1.astype(jnp.float32) * lk1.astype(jnp.float32)))
           - jnp.exp(jnp.sum(lq2.astype(jnp.float32) * lk2.astype(jnp.float32))) + lam_init)
    slopes = _alibi_slopes(DIFF_HEADS)
    k_pos = jnp.arange(s)
    k_chunk = k_pos // CHUNK
    n_blk = s // Q_BLOCK
    q_blocks = jnp.moveaxis(q.reshape(b, DIFF_HEADS, 2, n_blk, Q_BLOCK, hd), 3, 0)

    def block(args):
        qb, blk = args
        q_pos = blk * Q_BLOCK + jnp.arange(Q_BLOCK)
        scores = jnp.einsum('bhmqd,bhmkd->bhmqk', qb, k).astype(jnp.float32)
        dist = jnp.abs(q_pos[:, None] - k_pos[None, :]).astype(jnp.float32)
        bias = -slopes[:, None, None] * dist
        allowed = k_chunk[None, :] <= (q_pos // CHUNK)[:, None]
        scores = jnp.where(allowed, scores + bias[None, :, None], -jnp.inf)
        p = jax.nn.softmax(scores, axis=-1)
        a = p[:, :, 0] - lam * p[:, :, 1]
        o = jnp.einsum('bhqk,bhke->bhqe', a.astype(v.dtype), v)
        return _rms(o, subln) * (1.0 - lam_init)

    out = lax.map(block, (q_blocks, jnp.arange(n_blk)))
    out = jnp.transpose(out, (1, 0, 3, 2, 4)).reshape(b, s, D_MODEL)
    return out @ w_out


def _pool_mixer(h, w_in, w_group, scale, w_out):
    b, s, _ = h.shape
    u = (h @ w_in).reshape(b, s, POOL_GROUPS, POOL_GROUP_DIM)
    uf = u.astype(jnp.float32)
    cs = jnp.pad(jnp.cumsum(uf, axis=1), ((0, 0), (1, 0), (0, 0), (0, 0)))
    hi = jnp.arange(s) + 1
    pooled = []
    for g, w in enumerate(POOL_WINDOWS):
        lo = jnp.maximum(hi - w, 0)
        cnt = jnp.minimum(hi, w).astype(jnp.float32)
        pooled.append((cs[:, hi, g] - cs[:, lo, g]) / cnt[None, :, None])
    pooled = jnp.stack(pooled, axis=2) - uf
    y = jnp.einsum('bsgc,gcd->bsgd', pooled.astype(h.dtype), w_group)
    y = y.reshape(b, s, D_MODEL) * scale
    return y @ w_out


def _swiglu(h, w1, w3, w2):
    return (jax.nn.silu(h @ w1) * (h @ w3)) @ w2


def setup_inputs(seed: int = 0) -> dict:
    key = jax.random.key(seed)
    ks = iter(jax.random.split(key, 40))
    f32 = jnp.float32
    D = D_MODEL

    def nrm(shape, fan_in, gain=1.0):
        return jax.random.normal(next(ks), shape, f32) * (gain * fan_in ** -0.5)

    def gain_vec(shape):
        return 1.0 + 0.1 * jax.random.normal(next(ks), shape, f32)

    def small(shape, s=0.02):
        return s * jax.random.normal(next(ks), shape, f32)

    return {
        'x': jax.random.normal(next(ks), (BATCH, SEQ, D), f32),
        'c': jax.random.normal(next(ks), (BATCH, D), f32),
        'ada_w': nrm((DEPTH, D, N_MOD * D), D, 0.5),
        'ada_b': small((DEPTH, N_MOD * D)),
        'norm_mix': gain_vec((DEPTH, D)),
        'norm_ffn': gain_vec((DEPTH, D)),
        'ffn_w1': nrm((DEPTH, D, D_FF), D),
        'ffn_w3': nrm((DEPTH, D, D_FF), D),
        'ffn_w2': nrm((DEPTH, D_FF, D), D_FF),
        'conv_w_in': nrm((N_CONV_LAYERS, D, 3 * D), D),
        'conv_w': nrm((N_CONV_LAYERS, CONV_WIDTH, D), CONV_WIDTH),
        'conv_w_out': nrm((N_CONV_LAYERS, D, D), D),
        'diff_w_qkv': nrm((N_DIFF_LAYERS, D, 3 * D), D),
        'diff_q_norm': gain_vec((N_DIFF_LAYERS, DIFF_HEAD_DIM)),
        'diff_k_norm': gain_vec((N_DIFF_LAYERS, DIFF_HEAD_DIM)),
        'diff_lq1': small((N_DIFF_LAYERS, DIFF_HEAD_DIM), 0.1),
        'diff_lk1': small((N_DIFF_LAYERS, DIFF_HEAD_DIM), 0.1),
        'diff_lq2': small((N_DIFF_LAYERS, DIFF_HEAD_DIM), 0.1),
        'diff_lk2': small((N_DIFF_LAYERS, DIFF_HEAD_DIM), 0.1),
        'diff_subln': gain_vec((N_DIFF_LAYERS, 2 * DIFF_HEAD_DIM)),
        'diff_w_out': nrm((N_DIFF_LAYERS, D, D), D),
        'pool_w_in': nrm((N_POOL_LAYERS, D, D), D),
        'pool_w_group': nrm((N_POOL_LAYERS, POOL_GROUPS, POOL_GROUP_DIM, POOL_GROUP_DIM), POOL_GROUP_DIM),
        'pool_scale': gain_vec((N_POOL_LAYERS, D)),
        'pool_w_out': nrm((N_POOL_LAYERS, D, D), D),
    }


def reference(x, c, ada_w, ada_b, norm_mix, norm_ffn, ffn_w1, ffn_w3, ffn_w2,
              conv_w_in, conv_w, conv_w_out,
              diff_w_qkv, diff_q_norm, diff_k_norm, diff_lq1, diff_lk1, diff_lq2, diff_lk2,
              diff_subln, diff_w_out,
              pool_w_in, pool_w_group, pool_scale, pool_w_out):
    cond = jax.nn.silu(c)
    for i in range(DEPTH):
        kind = i % N_MIXERS
        j = i // N_MIXERS
        mod = cond @ ada_w[i] + ada_b[i]
        sh1, sc1, g1, sh2, sc2, g2 = jnp.split(mod, N_MOD, axis=-1)
        h = _rms(x, norm_mix[i]) * (1.0 + sc1[:, None]) + sh1[:, None]
        if kind == 0:
            y = _conv_mixer(h, conv_w_in[j], conv_w[j], conv_w_out[j])
        elif kind == 1:
            lam_init = 0.8 - 0.6 * math.exp(-0.3 * i)
            y = _diff_attention(h, diff_w_qkv[j], diff_q_norm[j], diff_k_norm[j],
                                diff_lq1[j], diff_lk1[j], diff_lq2[j], diff_lk2[j],
                                diff_subln[j], diff_w_out[j], lam_init)
        else:
            y = _pool_mixer(h, pool_w_in[j], pool_w_group[j], pool_scale[j], pool_w_out[j])
        x = x + g1[:, None] * y
        h = _rms(x, norm_ffn[i]) * (1.0 + sc2[:, None]) + sh2[:, None]
        x = x + g2[:, None] * _swiglu(h, ffn_w1[i], ffn_w3[i], ffn_w2[i])
    return x
```

```python
import functools
import math

import jax
import jax.numpy as jnp
from jax import lax
from jax.experimental import pallas as pl
from jax.experimental.pallas import tpu as pltpu

D_MODEL = 1024
DEPTH = 4
CHUNK = 64
N_MIXERS = 3
D_FF = 2816
CONV_WIDTH = 3
DIFF_HEADS = 8
DIFF_HEAD_DIM = 64
V_DIM = 2 * DIFF_HEAD_DIM
POOL_WINDOWS = (2, 4, 8, 16)
POOL_GROUP_DIM = D_MODEL // len(POOL_WINDOWS)
N_MOD = 6
EPS = 1e-6

SUBLANES = 8
LANES = 128
VMEM_LIMIT_BYTES = 56 * 1024 * 1024

TOKEN_TILE = 512
ADA_TILE = 2048
FFN_CHUNKS = 2
ATTN_TILE = 256
POOL_HALO = 16
CONV_HALO = 8

_BF16 = jnp.bfloat16
_F32 = jnp.float32
_NEG_BIG = -1e30


def _dot(a, b):
    return jnp.dot(a, b, preferred_element_type=_F32)


def _const_spec(shape):
    n = len(shape)
    return pl.BlockSpec(shape, lambda *_: (0,) * n, pipeline_mode=pl.Buffered(1))


def _params(n_axes):
    return pltpu.CompilerParams(
        dimension_semantics=("arbitrary",) * n_axes,
        vmem_limit_bytes=VMEM_LIMIT_BYTES)


def _mod_norm(x, gain, scale, shift):
    ms = jnp.mean(x * x, axis=-1, keepdims=True)
    y = x * lax.rsqrt(ms + EPS) * gain
    return y * (1.0 + scale) + shift


def _ada_kernel(c_ref, w_ref, b_ref, o_ref):
    c = c_ref[...]
    cond = c * jax.nn.sigmoid(c)
    o_ref[...] = _dot(cond.astype(_BF16), w_ref[...].astype(_BF16)) + b_ref[...]


def _ada_modulation(c, ada_w, ada_b):
    b, d = c.shape
    n = ada_w.shape[-1]
    return pl.pallas_call(
        _ada_kernel,
        grid=(DEPTH, n // ADA_TILE),
        in_specs=[
            pl.BlockSpec((b, d), lambda l, j: (0, 0)),
            pl.BlockSpec((None, d, ADA_TILE), lambda l, j: (l, 0, j)),
            pl.BlockSpec((None, 1, ADA_TILE), lambda l, j: (l, 0, j)),
        ],
        out_specs=pl.BlockSpec((None, b, ADA_TILE), lambda l, j: (l, 0, j)),
        out_shape=jax.ShapeDtypeStruct((DEPTH, b, n), _F32),
        compiler_params=_params(2),
        name="ada_modulation",
    )(c, ada_w, ada_b.reshape(DEPTH, 1, n))


def _ffn_kernel(x_ref, mod_ref, g_ref, w1_ref, w3_ref, w2_ref, o_ref):
    x = x_ref[...]
    h = _mod_norm(x, g_ref[...], mod_ref[4:5, :], mod_ref[3:4, :]).astype(_BF16)
    fc = D_FF // FFN_CHUNKS
    acc = None
    for c in range(FFN_CHUNKS):
        a = _dot(h, w1_ref[:, c * fc:(c + 1) * fc])
        b = _dot(h, w3_ref[:, c * fc:(c + 1) * fc])
        g = (a * jax.nn.sigmoid(a) * b).astype(_BF16)
        y = _dot(g, w2_ref[c * fc:(c + 1) * fc, :])
        acc = y if acc is None else acc + y
    o_ref[...] = x + mod_ref[5:6, :] * acc


def _token_spec():
    return pl.BlockSpec((None, TOKEN_TILE, D_MODEL), lambda b, i: (b, i, 0))


def _mod_spec():
    return pl.BlockSpec((None, N_MOD, D_MODEL), lambda b, i: (b, 0, 0))


def _ffn_layer(x, mod, gain, w1, w3, w2):
    b, s, d = x.shape
    return pl.pallas_call(
        _ffn_kernel,
        grid=(b, s // TOKEN_TILE),
        in_specs=[_token_spec(), _mod_spec(), _const_spec((1, d)),
                  _const_spec(w1.shape), _const_spec(w3.shape),
                  _const_spec(w2.shape)],
        out_specs=_token_spec(),
        out_shape=jax.ShapeDtypeStruct(x.shape, x.dtype),
        compiler_params=_params(2),
        name="ffn",
    )(x, mod, gain, w1, w3, w2)


def _conv_kernel(x_ref, mod_ref, g_ref, win_ref, cw_ref, wout_ref, o_ref, ubuf):
    d = D_MODEL
    x = x_ref[...]
    h = _mod_norm(x, g_ref[...], mod_ref[1:2, :], mod_ref[0:1, :]).astype(_BF16)
    bg = _dot(h, win_ref[:, 0:d])
    cg = _dot(h, win_ref[:, d:2 * d])
    xv = _dot(h, win_ref[:, 2 * d:3 * d])
    u = cg * xv

    @pl.when(pl.program_id(1) == 0)
    def _():
        ubuf[0:CONV_HALO, :] = jnp.zeros((CONV_HALO, d), _F32)

    ubuf[CONV_HALO:CONV_HALO + TOKEN_TILE, :] = u
    u1 = ubuf[CONV_HALO - 1:CONV_HALO - 1 + TOKEN_TILE, :]
    u2 = ubuf[CONV_HALO - 2:CONV_HALO - 2 + TOKEN_TILE, :]
    y = cw_ref[0:1, :] * u2 + cw_ref[1:2, :] * u1 + cw_ref[2:3, :] * u
    ubuf[0:CONV_HALO, :] = ubuf[TOKEN_TILE:TOKEN_TILE + CONV_HALO, :]
    out = _dot((bg * y).astype(_BF16), wout_ref[...])
    o_ref[...] = x + mod_ref[2:3, :] * out


def _conv_layer(x, mod, gain, w_in, conv_w, w_out):
    b, s, d = x.shape
    return pl.pallas_call(
        _conv_kernel,
        grid=(b, s // TOKEN_TILE),
        in_specs=[_token_spec(), _mod_spec(), _const_spec((1, d)),
                  _const_spec(w_in.shape), _const_spec(conv_w.shape),
                  _const_spec(w_out.shape)],
        out_specs=_token_spec(),
        out_shape=jax.ShapeDtypeStruct(x.shape, x.dtype),
        scratch_shapes=[pltpu.VMEM((TOKEN_TILE + CONV_HALO, d), _F32)],
        compiler_params=_params(2),
        name="conv_mixer",
    )(x, mod, gain, w_in, conv_w, w_out)


def _pool_kernel(x_ref, mod_ref, g_ref, win_ref, wg_ref, sc_ref, wout_ref,
                 o_ref, ubuf):
    d = D_MODEL
    gd = POOL_GROUP_DIM
    i = pl.program_id(1)
    x = x_ref[...]
    h = _mod_norm(x, g_ref[...], mod_ref[1:2, :], mod_ref[0:1, :]).astype(_BF16)
    u = _dot(h, win_ref[...])

    @pl.when(i == 0)
    def _():
        ubuf[0:POOL_HALO, :] = jnp.zeros((POOL_HALO, d), _F32)

    ubuf[POOL_HALO:POOL_HALO + TOKEN_TILE, :] = u
    pos = i * TOKEN_TILE + lax.broadcasted_iota(jnp.int32, (TOKEN_TILE, 1), 0)
    ys = []
    for g, w in enumerate(POOL_WINDOWS):
        cols = slice(g * gd, (g + 1) * gd)
        ug = u[:, cols]
        total = ug
        for k in range(1, w):
            total = total + ubuf[POOL_HALO - k:POOL_HALO - k + TOKEN_TILE, cols]
        cnt = jnp.minimum(pos + 1, w).astype(_F32)
        pooled = total / cnt - ug
        ys.append(_dot(pooled.astype(_BF16), wg_ref[g]))
    ubuf[0:POOL_HALO, :] = ubuf[TOKEN_TILE:TOKEN_TILE + POOL_HALO, :]
    y = jnp.concatenate(ys, axis=-1) * sc_ref[...]
    out = _dot(y.astype(_BF16), wout_ref[...])
    o_ref[...] = x + mod_ref[2:3, :] * out


def _pool_layer(x, mod, gain, w_in, w_group, scale, w_out):
    b, s, d = x.shape
    return pl.pallas_call(
        _pool_kernel,
        grid=(b, s // TOKEN_TILE),
        in_specs=[_token_spec(), _mod_spec(), _const_spec((1, d)),
                  _const_spec(w_in.shape), _const_spec(w_group.shape),
                  _const_spec((1, d)), _const_spec(w_out.shape)],
        out_specs=_token_spec(),
        out_shape=jax.ShapeDtypeStruct(x.shape, x.dtype),
        scratch_shapes=[pltpu.VMEM((TOKEN_TILE + POOL_HALO, d), _F32)],
        compiler_params=_params(2),
        name="pool_mixer",
    )(x, mod, gain, w_in, w_group, scale, w_out)


def _group_rms_scale(t, e_ref, et_ref):
    sq = t * t
    hi = sq.astype(_BF16)
    lo = (sq - hi.astype(_F32)).astype(_BF16)
    ss = _dot(hi, e_ref[...]) + _dot(lo, e_ref[...])
    inv = lax.rsqrt(ss * (1.0 / DIFF_HEAD_DIM) + EPS)
    ihi = inv.astype(_BF16)
    ilo = (inv - ihi.astype(_F32)).astype(_BF16)
    return _dot(ihi, et_ref[...]) + _dot(ilo, et_ref[...])


def _qkv_kernel(x_ref, mod_ref, g_ref, w_ref, e_ref, et_ref, gq_ref, gk_ref,
                q_ref, k_ref, v_ref):
    d = D_MODEL
    x = x_ref[...]
    h = _mod_norm(x, g_ref[...], mod_ref[1:2, :], mod_ref[0:1, :]).astype(_BF16)
    q = _dot(h, w_ref[:, 0:d])
    k = _dot(h, w_ref[:, d:2 * d])
    v = _dot(h, w_ref[:, 2 * d:3 * d])
    qn = q * _group_rms_scale(q, e_ref, et_ref) * gq_ref[...] * (DIFF_HEAD_DIM ** -0.5)
    kn = k * _group_rms_scale(k, e_ref, et_ref) * gk_ref[...]
    first_half = (lax.broadcasted_iota(jnp.int32, (1, d), 1) % V_DIM) < DIFF_HEAD_DIM
    q_ref[0] = jnp.where(first_half, qn, 0.0).astype(_BF16)
    q_ref[1] = jnp.where(first_half, 0.0, qn).astype(_BF16)
    k_ref[...] = kn.astype(_BF16)
    v_ref[...] = v.astype(_BF16)


def _qkv_layer(x, mod, gain, w_qkv, gq, gk):
    b, s, d = x.shape
    groups = d // DIFF_HEAD_DIM
    col_group = jnp.arange(d, dtype=jnp.int32) // DIFF_HEAD_DIM
    e = (col_group[:, None] == jnp.arange(LANES, dtype=jnp.int32)[None, :])
    e = e.astype(_BF16)
    gq_t = jnp.tile(gq, groups).reshape(1, d)
    gk_t = jnp.tile(gk, groups).reshape(1, d)
    kv_spec = pl.BlockSpec((None, TOKEN_TILE, d), lambda bi, i: (bi, i, 0))
    return pl.pallas_call(
        _qkv_kernel,
        grid=(b, s // TOKEN_TILE),
        in_specs=[_token_spec(), _mod_spec(), _const_spec((1, d)),
                  _const_spec(w_qkv.shape), _const_spec((d, LANES)),
                  _const_spec((LANES, d)), _const_spec((1, d)),
                  _const_spec((1, d))],
        out_specs=[
            pl.BlockSpec((None, 2, TOKEN_TILE, d), lambda bi, i: (bi, 0, i, 0)),
            kv_spec, kv_spec],
        out_shape=[jax.ShapeDtypeStruct((b, 2, s, d), _BF16),
                   jax.ShapeDtypeStruct((b, s, d), _BF16),
                   jax.ShapeDtypeStruct((b, s, d), _BF16)],
        compiler_params=_params(2),
        name="diff_qkv",
    )(x, mod, gain, w_qkv, e, e.T, gq_t, gk_t)


def _flash_kernel(slopes_ref, q_ref, k_ref, v_ref, lam_ref, subln_ref, o_ref,
                  m_sc, l_sc, acc_sc, *, lam_init):
    t = ATTN_TILE
    head = pl.program_id(1)
    qi = pl.program_id(2)
    slope = slopes_ref[head]
    q2 = q_ref[...].reshape(2 * t, V_DIM)
    row = lax.broadcasted_iota(jnp.int32, (2 * t, t), 0) % t
    col = lax.broadcasted_iota(jnp.int32, (2 * t, t), 1)
    rel = (row - col).astype(_F32)

    def scores(j):
        kb = k_ref[pl.ds(pl.multiple_of(j * t, t), t), :]
        return lax.dot_general(q2, kb, (((1,), (1,)), ((), ())),
                               preferred_element_type=_F32)

    def update(j, s):
        vb = v_ref[pl.ds(pl.multiple_of(j * t, t), t), :]
        m_prev = m_sc[...]
        m_next = jnp.maximum(m_prev, jnp.max(s, axis=-1, keepdims=True))
        alpha = jnp.exp(m_prev - m_next)
        p = jnp.exp(s - m_next)
        l_sc[...] = alpha * l_sc[...] + jnp.sum(p, axis=-1, keepdims=True)
        acc_sc[...] = alpha * acc_sc[...] + _dot(p.astype(_BF16), vb)
        m_sc[...] = m_next

    m_sc[...] = jnp.full(m_sc.shape, _NEG_BIG, _F32)
    l_sc[...] = jnp.zeros(l_sc.shape, _F32)
    acc_sc[...] = jnp.zeros(acc_sc.shape, _F32)

    def body(j, carry):
        offset = ((qi - j) * t).astype(_F32)
        update(j, scores(j) - slope * (rel + offset))
        return carry

    lax.fori_loop(0, qi, body, 0)

    allowed = (col // CHUNK) <= (row // CHUNK)
    s = scores(qi) - slope * jnp.abs(rel)
    update(qi, jnp.where(allowed, s, _NEG_BIG))

    lam_rows = lam_ref[...]
    dot1 = jnp.sum(lam_rows[0:1, :] * lam_rows[1:2, :], axis=-1, keepdims=True)
    dot2 = jnp.sum(lam_rows[2:3, :] * lam_rows[3:4, :], axis=-1, keepdims=True)
    lam = jnp.exp(dot1) - jnp.exp(dot2) + lam_init
    o0 = acc_sc[0:t, :] / l_sc[0:t, :]
    o1 = acc_sc[t:2 * t, :] / l_sc[t:2 * t, :]
    o = o0 - lam * o1
    o = o * lax.rsqrt(jnp.mean(o * o, axis=-1, keepdims=True) + EPS)
    o_ref[...] = (o * subln_ref[...] * (1.0 - lam_init)).astype(o_ref.dtype)


def _flash_layer(q, k, v, lam_rows, subln, lam_init):
    b, _, s, d = q.shape
    t = ATTN_TILE
    slopes = 2.0 ** (-8.0 * jnp.arange(1, DIFF_HEADS + 1, dtype=_F32) / DIFF_HEADS)
    kv_spec = pl.BlockSpec((None, s, V_DIM), lambda bi, h, i, sl: (bi, 0, h))
    grid_spec = pltpu.PrefetchScalarGridSpec(
        num_scalar_prefetch=1,
        grid=(b, DIFF_HEADS, s // t),
        in_specs=[
            pl.BlockSpec((None, 2, t, V_DIM), lambda bi, h, i, sl: (bi, 0, i, h)),
            kv_spec, kv_spec,
            pl.BlockSpec(lam_rows.shape, lambda bi, h, i, sl: (0, 0)),
            pl.BlockSpec((1, V_DIM), lambda bi, h, i, sl: (0, 0)),
        ],
        out_specs=pl.BlockSpec((None, t, V_DIM), lambda bi, h, i, sl: (bi, i, h)),
        scratch_shapes=[pltpu.VMEM((2 * t, 1), _F32), pltpu.VMEM((2 * t, 1), _F32),
                        pltpu.VMEM((2 * t, V_DIM), _F32)],
    )
    return pl.pallas_call(
        functools.partial(_flash_kernel, lam_init=lam_init),
        grid_spec=grid_spec,
        out_shape=jax.ShapeDtypeStruct((b, s, d), _BF16),
        compiler_params=_params(3),
        name="diff_flash",
    )(slopes, q, k, v, lam_rows, subln)


def _proj_kernel(x_ref, mod_ref, a_ref, w_ref, o_ref):
    o_ref[...] = x_ref[...] + mod_ref[2:3, :] * _dot(a_ref[...], w_ref[...])


def _proj_layer(x, mod, a, w_out):
    b, s, d = x.shape
    return pl.pallas_call(
        _proj_kernel,
        grid=(b, s // TOKEN_TILE),
        in_specs=[_token_spec(), _mod_spec(), _token_spec(),
                  _const_spec(w_out.shape)],
        out_specs=_token_spec(),
        out_shape=jax.ShapeDtypeStruct(x.shape, x.dtype),
        compiler_params=_params(2),
        name="diff_out_proj",
    )(x, mod, a, w_out)


def kernel(x, c, ada_w, ada_b, norm_mix, norm_ffn, ffn_w1, ffn_w3, ffn_w2, conv_w_in, conv_w, conv_w_out, diff_w_qkv, diff_q_norm, diff_k_norm, diff_lq1, diff_lk1, diff_lq2, diff_lk2, diff_subln, diff_w_out, pool_w_in, pool_w_group, pool_scale, pool_w_out):
    b = x.shape[0]
    bf = lambda w: w.astype(_BF16)
    mod_all = _ada_modulation(c, ada_w, ada_b).reshape(DEPTH, b, N_MOD, D_MODEL)
    for i in range(DEPTH):
        kind = i % N_MIXERS
        j = i // N_MIXERS
        mod = mod_all[i]
        gain = norm_mix[i].reshape(1, D_MODEL)
        if kind == 0:
            x = _conv_layer(x, mod, gain, bf(conv_w_in[j]), conv_w[j],
                            bf(conv_w_out[j]))
        elif kind == 1:
            lam_init = 0.8 - 0.6 * math.exp(-0.3 * i)
            q, k, v = _qkv_layer(x, mod, gain, bf(diff_w_qkv[j]),
                                 diff_q_norm[j], diff_k_norm[j])
            lam_rows = jnp.stack([diff_lq1[j], diff_lk1[j], diff_lq2[j], diff_lk2[j]])
            a = _flash_layer(q, k, v, lam_rows, diff_subln[j].reshape(1, V_DIM),
                             lam_init)
            x = _proj_layer(x, mod, a, bf(diff_w_out[j]))
        else:
            x = _pool_layer(x, mod, gain, bf(pool_w_in[j]), bf(pool_w_group[j]),
                            pool_scale[j].reshape(1, D_MODEL), bf(pool_w_out[j]))
        x = _ffn_layer(x, mod, norm_ffn[i].reshape(1, D_MODEL), bf(ffn_w1[i]),
                       bf(ffn_w3[i]), bf(ffn_w2[i]))
    return x
```

```python
import functools
import math

import jax
import jax.numpy as jnp
from jax import lax
from jax.experimental import pallas as pl
from jax.experimental.pallas import tpu as pltpu

D_MODEL = 1024
DEPTH = 4
CHUNK = 64
N_MIXERS = 3
D_FF = 2816
CONV_WIDTH = 3
DIFF_HEADS = 8
DIFF_HEAD_DIM = 64
V_DIM = 2 * DIFF_HEAD_DIM
POOL_WINDOWS = (2, 4, 8, 16)
POOL_GROUP_DIM = D_MODEL // len(POOL_WINDOWS)
N_MOD = 6
EPS = 1e-6

SUBLANES = 8
LANES = 128
VMEM_LIMIT_BYTES = 56 * 1024 * 1024

TOKEN_TILE = 512
ADA_TILE = 2048
FFN_CHUNKS = 2
ATTN_TILE = 256
HEADS_PER_STEP = 8
BF16_SUBLANES = 16
VT_ROWS = V_DIM + BF16_SUBLANES
LOG2_E = math.log2(math.e)
POOL_HALO = 16
CONV_HALO = 8

_BF16 = jnp.bfloat16
_F32 = jnp.float32
_NEG_BIG = -1e30


def _dot(a, b):
    return jnp.dot(a, b, preferred_element_type=_F32)


def _const_spec(shape):
    n = len(shape)
    return pl.BlockSpec(shape, lambda *_: (0,) * n, pipeline_mode=pl.Buffered(1))


def _params(n_axes):
    return pltpu.CompilerParams(
        dimension_semantics=("arbitrary",) * n_axes,
        vmem_limit_bytes=VMEM_LIMIT_BYTES)


def _mod_norm(x, gain, scale, shift):
    ms = jnp.mean(x * x, axis=-1, keepdims=True)
    y = x * lax.rsqrt(ms + EPS) * gain
    return y * (1.0 + scale) + shift


def _ada_kernel(c_ref, w_ref, b_ref, o_ref):
    c = c_ref[...]
    cond = c * jax.nn.sigmoid(c)
    o_ref[...] = _dot(cond.astype(_BF16), w_ref[...].astype(_BF16)) + b_ref[...]


def _ada_modulation(c, ada_w, ada_b):
    b, d = c.shape
    n = ada_w.shape[-1]
    return pl.pallas_call(
        _ada_kernel,
        grid=(DEPTH, n // ADA_TILE),
        in_specs=[
            pl.BlockSpec((b, d), lambda l, j: (0, 0)),
            pl.BlockSpec((None, d, ADA_TILE), lambda l, j: (l, 0, j)),
            pl.BlockSpec((None, 1, ADA_TILE), lambda l, j: (l, 0, j)),
        ],
        out_specs=pl.BlockSpec((None, b, ADA_TILE), lambda l, j: (l, 0, j)),
        out_shape=jax.ShapeDtypeStruct((DEPTH, b, n), _F32),
        compiler_params=_params(2),
        name="ada_modulation",
    )(c, ada_w, ada_b.reshape(DEPTH, 1, n))


def _ffn_kernel(x_ref, mod_ref, g_ref, w1_ref, w3_ref, w2_ref, o_ref):
    x = x_ref[...]
    h = _mod_norm(x, g_ref[...], mod_ref[4:5, :], mod_ref[3:4, :]).astype(_BF16)
    fc = D_FF // FFN_CHUNKS
    acc = None
    for c in range(FFN_CHUNKS):
        a = _dot(h, w1_ref[:, c * fc:(c + 1) * fc])
        b = _dot(h, w3_ref[:, c * fc:(c + 1) * fc])
        g = (a * jax.nn.sigmoid(a) * b).astype(_BF16)
        y = _dot(g, w2_ref[c * fc:(c + 1) * fc, :])
        acc = y if acc is None else acc + y
    o_ref[...] = x + mod_ref[5:6, :] * acc


def _token_spec():
    return pl.BlockSpec((None, TOKEN_TILE, D_MODEL), lambda b, i: (b, i, 0))


def _mod_spec():
    return pl.BlockSpec((None, N_MOD, D_MODEL), lambda b, i: (b, 0, 0))


def _ffn_layer(x, mod, gain, w1, w3, w2):
    b, s, d = x.shape
    return pl.pallas_call(
        _ffn_kernel,
        grid=(b, s // TOKEN_TILE),
        in_specs=[_token_spec(), _mod_spec(), _const_spec((1, d)),
                  _const_spec(w1.shape), _const_spec(w3.shape),
                  _const_spec(w2.shape)],
        out_specs=_token_spec(),
        out_shape=jax.ShapeDtypeStruct(x.shape, x.dtype),
        compiler_params=_params(2),
        name="ffn",
    )(x, mod, gain, w1, w3, w2)


def _conv_kernel(x_ref, mod_ref, g_ref, win_ref, cw_ref, wout_ref, o_ref, ubuf):
    d = D_MODEL
    x = x_ref[...]
    h = _mod_norm(x, g_ref[...], mod_ref[1:2, :], mod_ref[0:1, :]).astype(_BF16)
    bg = _dot(h, win_ref[:, 0:d])
    cg = _dot(h, win_ref[:, d:2 * d])
    xv = _dot(h, win_ref[:, 2 * d:3 * d])
    u = cg * xv

    @pl.when(pl.program_id(1) == 0)
    def _():
        ubuf[0:CONV_HALO, :] = jnp.zeros((CONV_HALO, d), _F32)

    ubuf[CONV_HALO:CONV_HALO + TOKEN_TILE, :] = u
    u1 = ubuf[CONV_HALO - 1:CONV_HALO - 1 + TOKEN_TILE, :]
    u2 = ubuf[CONV_HALO - 2:CONV_HALO - 2 + TOKEN_TILE, :]
    y = cw_ref[0:1, :] * u2 + cw_ref[1:2, :] * u1 + cw_ref[2:3, :] * u
    ubuf[0:CONV_HALO, :] = ubuf[TOKEN_TILE:TOKEN_TILE + CONV_HALO, :]
    out = _dot((bg * y).astype(_BF16), wout_ref[...])
    o_ref[...] = x + mod_ref[2:3, :] * out


def _conv_layer(x, mod, gain, w_in, conv_w, w_out):
    b, s, d = x.shape
    return pl.pallas_call(
        _conv_kernel,
        grid=(b, s // TOKEN_TILE),
        in_specs=[_token_spec(), _mod_spec(), _const_spec((1, d)),
                  _const_spec(w_in.shape), _const_spec(conv_w.shape),
                  _const_spec(w_out.shape)],
        out_specs=_token_spec(),
        out_shape=jax.ShapeDtypeStruct(x.shape, x.dtype),
        scratch_shapes=[pltpu.VMEM((TOKEN_TILE + CONV_HALO, d), _F32)],
        compiler_params=_params(2),
        name="conv_mixer",
    )(x, mod, gain, w_in, conv_w, w_out)


def _pool_kernel(x_ref, mod_ref, g_ref, win_ref, wg_ref, sc_ref, wout_ref,
                 o_ref, ubuf):
    d = D_MODEL
    gd = POOL_GROUP_DIM
    i = pl.program_id(1)
    x = x_ref[...]
    h = _mod_norm(x, g_ref[...], mod_ref[1:2, :], mod_ref[0:1, :]).astype(_BF16)
    u = _dot(h, win_ref[...])

    @pl.when(i == 0)
    def _():
        ubuf[0:POOL_HALO, :] = jnp.zeros((POOL_HALO, d), _F32)

    ubuf[POOL_HALO:POOL_HALO + TOKEN_TILE, :] = u
    pos = i * TOKEN_TILE + lax.broadcasted_iota(jnp.int32, (TOKEN_TILE, 1), 0)
    ys = []
    for g, w in enumerate(POOL_WINDOWS):
        cols = slice(g * gd, (g + 1) * gd)
        ug = u[:, cols]
        total = ug
        for k in range(1, w):
            total = total + ubuf[POOL_HALO - k:POOL_HALO - k + TOKEN_TILE, cols]
        cnt = jnp.minimum(pos + 1, w).astype(_F32)
        pooled = total / cnt - ug
        ys.append(_dot(pooled.astype(_BF16), wg_ref[g]))
    ubuf[0:POOL_HALO, :] = ubuf[TOKEN_TILE:TOKEN_TILE + POOL_HALO, :]
    y = jnp.concatenate(ys, axis=-1) * sc_ref[...]
    out = _dot(y.astype(_BF16), wout_ref[...])
    o_ref[...] = x + mod_ref[2:3, :] * out


def _pool_layer(x, mod, gain, w_in, w_group, scale, w_out):
    b, s, d = x.shape
    return pl.pallas_call(
        _pool_kernel,
        grid=(b, s // TOKEN_TILE),
        in_specs=[_token_spec(), _mod_spec(), _const_spec((1, d)),
                  _const_spec(w_in.shape), _const_spec(w_group.shape),
                  _const_spec((1, d)), _const_spec(w_out.shape)],
        out_specs=_token_spec(),
        out_shape=jax.ShapeDtypeStruct(x.shape, x.dtype),
        scratch_shapes=[pltpu.VMEM((TOKEN_TILE + POOL_HALO, d), _F32)],
        compiler_params=_params(2),
        name="pool_mixer",
    )(x, mod, gain, w_in, w_group, scale, w_out)


def _group_rms_scale(t, e_ref, et_ref):
    sq = t * t
    hi = sq.astype(_BF16)
    lo = (sq - hi.astype(_F32)).astype(_BF16)
    ss = _dot(hi, e_ref[...]) + _dot(lo, e_ref[...])
    inv = lax.rsqrt(ss * (1.0 / DIFF_HEAD_DIM) + EPS)
    ihi = inv.astype(_BF16)
    ilo = (inv - ihi.astype(_F32)).astype(_BF16)
    return _dot(ihi, et_ref[...]) + _dot(ilo, et_ref[...])


def _qkv_kernel(x_ref, mod_ref, g_ref, w_ref, e_ref, et_ref, gq_ref, gk_ref,
                q_ref, k_ref, vt_ref):
    d = D_MODEL
    x = x_ref[...]
    h = _mod_norm(x, g_ref[...], mod_ref[1:2, :], mod_ref[0:1, :]).astype(_BF16)
    q = _dot(h, w_ref[:, 0:d])
    k = _dot(h, w_ref[:, d:2 * d])
    v = _dot(h, w_ref[:, 2 * d:3 * d])
    q_scale = DIFF_HEAD_DIM ** -0.5 * LOG2_E
    qn = q * _group_rms_scale(q, e_ref, et_ref) * gq_ref[...] * q_scale
    kn = k * _group_rms_scale(k, e_ref, et_ref) * gk_ref[...]
    q_ref[...] = qn.astype(_BF16)
    k_ref[...] = kn.astype(_BF16)
    t = ATTN_TILE
    pad_rows = VT_ROWS - V_DIM
    ones_row = (lax.broadcasted_iota(jnp.int32, (pad_rows, t), 0) == 0)
    ones_row = ones_row.astype(_F32).astype(_BF16)
    for hd in range(DIFF_HEADS):
        for blk in range(TOKEN_TILE // t):
            vb = v[blk * t:(blk + 1) * t, hd * V_DIM:(hd + 1) * V_DIM]
            vt_ref[hd, blk, 0:V_DIM, :] = vb.T.astype(_BF16)
            vt_ref[hd, blk, V_DIM:VT_ROWS, :] = ones_row


def _qkv_layer(x, mod, gain, w_qkv, gq, gk):
    b, s, d = x.shape
    t = ATTN_TILE
    groups = d // DIFF_HEAD_DIM
    col_group = jnp.arange(d, dtype=jnp.int32) // DIFF_HEAD_DIM
    e = (col_group[:, None] == jnp.arange(LANES, dtype=jnp.int32)[None, :])
    e = e.astype(_BF16)
    gq_t = jnp.tile(gq, groups).reshape(1, d)
    gk_t = jnp.tile(gk, groups).reshape(1, d)
    qk_spec = pl.BlockSpec((None, TOKEN_TILE, d), lambda bi, i: (bi, i, 0))
    blocks = TOKEN_TILE // t
    return pl.pallas_call(
        _qkv_kernel,
        grid=(b, s // TOKEN_TILE),
        in_specs=[_token_spec(), _mod_spec(), _const_spec((1, d)),
                  _const_spec(w_qkv.shape), _const_spec((d, LANES)),
                  _const_spec((LANES, d)), _const_spec((1, d)),
                  _const_spec((1, d))],
        out_specs=[
            qk_spec, qk_spec,
            pl.BlockSpec((None, DIFF_HEADS, blocks, VT_ROWS, t),
                         lambda bi, i: (bi, 0, i, 0, 0))],
        out_shape=[jax.ShapeDtypeStruct((b, s, d), _BF16),
                   jax.ShapeDtypeStruct((b, s, d), _BF16),
                   jax.ShapeDtypeStruct((b, DIFF_HEADS, s // t, VT_ROWS, t), _BF16)],
        compiler_params=_params(2),
        name="diff_qkv",
    )(x, mod, gain, w_qkv, e, e.T, gq_t, gk_t)


def _split3(x):
    hi = x.astype(_BF16).astype(_F32)
    r = x - hi
    mid = r.astype(_BF16).astype(_F32)
    lo = (r - mid).astype(_BF16).astype(_F32)
    return hi, mid, lo


def _flash_kernel(slopes_ref, q_ref, k_ref, vt_ref, lam_ref, subln_ref, o_ref,
                  kaug, dtile, acc_sc, *, lam_init):
    t = ATTN_TILE
    n_blk = kaug.shape[1]
    hp = pl.program_id(1)
    qi = pl.program_id(2)
    lane = lax.broadcasted_iota(jnp.int32, (1, V_DIM), 1)

    @pl.when(qi == 0)
    def _build_tables():
        key = lax.broadcasted_iota(jnp.int32, (t, t), 0)
        qry = lax.broadcasted_iota(jnp.int32, (t, t), 1)
        behind = jnp.minimum(qry - key, 0).astype(_F32)
        allowed = (key // CHUNK) <= (qry // CHUNK)
        key_pos = lax.broadcasted_iota(jnp.int32, (t, 1), 0).astype(_F32)
        for hh in range(HEADS_PER_STEP):
            sl = slopes_ref[hp * HEADS_PER_STEP + hh]
            hi, mid, lo = _split3(key_pos * sl)
            aug = jnp.where(lane == 0, hi, jnp.where(lane == 1, mid,
                            jnp.where(lane == 2, lo, 0.0))).astype(_BF16)
            for blk in range(n_blk):
                kaug[hh, blk, :, 0:V_DIM] = k_ref[blk * t:(blk + 1) * t,
                                                  hh * V_DIM:(hh + 1) * V_DIM]
                kaug[hh, blk, :, V_DIM:2 * V_DIM] = aug
            dtile[hh] = jnp.where(allowed, (2.0 * sl) * behind, _NEG_BIG)

    ones_aug = jnp.where(lane < 3, 1.0, 0.0)
    q2 = []
    for hh in range(HEADS_PER_STEP):
        qh = q_ref[:, hh * V_DIM:(hh + 1) * V_DIM].astype(_F32)
        ones = jnp.broadcast_to(ones_aug, qh.shape)
        q_map0 = jnp.concatenate([jnp.where(lane < DIFF_HEAD_DIM, qh, 0.0), ones], axis=1)
        q_map1 = jnp.concatenate([jnp.where(lane < DIFF_HEAD_DIM, 0.0, qh), ones], axis=1)
        q2.append(jnp.concatenate([q_map0, q_map1], axis=0).astype(_BF16))
        acc_sc[hh] = jnp.zeros(acc_sc.shape[1:], _F32)

    def scores(hh, j):
        return lax.dot_general(kaug[hh, j], q2[hh], (((1,), (1,)), ((), ())),
                               preferred_element_type=_F32)

    def update(hh, j, s, m_prev, shift):
        m_next = jnp.maximum(m_prev, jnp.max(s, axis=0, keepdims=True) + shift)
        alpha = jnp.exp2(m_prev - m_next)
        p = jnp.exp2(s - (m_next - shift)).astype(_BF16)
        acc_sc[hh] = alpha * acc_sc[hh] + _dot(vt_ref[hh, j], p)
        return m_next

    def body(j, ms):
        rel_blocks = ((j - qi) * t).astype(_F32)
        ss = [scores(hh, j) for hh in range(HEADS_PER_STEP)]
        return tuple(
            update(hh, j, ss[hh], ms[hh],
                   slopes_ref[hp * HEADS_PER_STEP + hh] * rel_blocks)
            for hh in range(HEADS_PER_STEP))

    m0 = tuple(jnp.full((1, 2 * t), _NEG_BIG, _F32) for _ in range(HEADS_PER_STEP))
    ms = lax.fori_loop(0, qi, body, m0)

    lam_rows = lam_ref[...]
    dot1 = jnp.sum(lam_rows[0:1, :] * lam_rows[1:2, :], axis=-1, keepdims=True)
    dot2 = jnp.sum(lam_rows[2:3, :] * lam_rows[3:4, :], axis=-1, keepdims=True)
    lam = jnp.exp(dot1) - jnp.exp(dot2) + lam_init

    ss = [scores(hh, qi) for hh in range(HEADS_PER_STEP)]
    for hh in range(HEADS_PER_STEP):
        d_both = jnp.concatenate([dtile[hh], dtile[hh]], axis=1)
        update(hh, qi, ss[hh] + d_both, ms[hh], 0.0)
    for hh in range(HEADS_PER_STEP):
        acc = acc_sc[hh]
        o_both = acc[0:V_DIM, :] * (1.0 / acc[V_DIM:V_DIM + 1, :])
        o = o_both[:, 0:t] - lam * o_both[:, t:2 * t]
        o = o * lax.rsqrt(jnp.mean(o * o, axis=0, keepdims=True) + EPS)
        o = o * (subln_ref[...] * (1.0 - lam_init))
        o_ref[:, hh * V_DIM:(hh + 1) * V_DIM] = o.T.astype(o_ref.dtype)


def _flash_layer(q, k, vt, lam_rows, subln, lam_init):
    b, s, d = q.shape
    t = ATTN_TILE
    hps = HEADS_PER_STEP
    slopes = 2.0 ** (-8.0 * jnp.arange(1, DIFF_HEADS + 1, dtype=_F32) / DIFF_HEADS)
    grid_spec = pltpu.PrefetchScalarGridSpec(
        num_scalar_prefetch=1,
        grid=(b, DIFF_HEADS // hps, s // t),
        in_specs=[
            pl.BlockSpec((None, t, hps * V_DIM), lambda bi, h, i, sl: (bi, i, h)),
            pl.BlockSpec((None, s, hps * V_DIM), lambda bi, h, i, sl: (bi, 0, h),
                         pipeline_mode=pl.Buffered(1)),
            pl.BlockSpec((None, hps, s // t, VT_ROWS, t),
                         lambda bi, h, i, sl: (bi, h, 0, 0, 0),
                         pipeline_mode=pl.Buffered(1)),
            pl.BlockSpec(lam_rows.shape, lambda bi, h, i, sl: (0, 0)),
            pl.BlockSpec((V_DIM, 1), lambda bi, h, i, sl: (0, 0)),
        ],
        out_specs=pl.BlockSpec((None, t, hps * V_DIM), lambda bi, h, i, sl: (bi, i, h)),
        scratch_shapes=[pltpu.VMEM((hps, s // t, t, 2 * V_DIM), _BF16),
                        pltpu.VMEM((hps, t, t), _F32),
                        pltpu.VMEM((hps, VT_ROWS, 2 * t), _F32)],
    )
    return pl.pallas_call(
        functools.partial(_flash_kernel, lam_init=lam_init),
        grid_spec=grid_spec,
        out_shape=jax.ShapeDtypeStruct((b, s, d), _BF16),
        compiler_params=_params(3),
        name="diff_flash",
    )(slopes * LOG2_E, q, k, vt, lam_rows, subln)


def _proj_kernel(x_ref, mod_ref, a_ref, w_ref, o_ref):
    o_ref[...] = x_ref[...] + mod_ref[2:3, :] * _dot(a_ref[...], w_ref[...])


def _proj_layer(x, mod, a, w_out):
    b, s, d = x.shape
    return pl.pallas_call(
        _proj_kernel,
        grid=(b, s // TOKEN_TILE),
        in_specs=[_token_spec(), _mod_spec(), _token_spec(),
                  _const_spec(w_out.shape)],
        out_specs=_token_spec(),
        out_shape=jax.ShapeDtypeStruct(x.shape, x.dtype),
        compiler_params=_params(2),
        name="diff_out_proj",
    )(x, mod, a, w_out)


def kernel(x, c, ada_w, ada_b, norm_mix, norm_ffn, ffn_w1, ffn_w3, ffn_w2, conv_w_in, conv_w, conv_w_out, diff_w_qkv, diff_q_norm, diff_k_norm, diff_lq1, diff_lk1, diff_lq2, diff_lk2, diff_subln, diff_w_out, pool_w_in, pool_w_group, pool_scale, pool_w_out):
    b = x.shape[0]
    bf = lambda w: w.astype(_BF16)
    mod_all = _ada_modulation(c, ada_w, ada_b).reshape(DEPTH, b, N_MOD, D_MODEL)
    for i in range(DEPTH):
        kind = i % N_MIXERS
        j = i // N_MIXERS
        mod = mod_all[i]
        gain = norm_mix[i].reshape(1, D_MODEL)
        if kind == 0:
            x = _conv_layer(x, mod, gain, bf(conv_w_in[j]), conv_w[j],
                            bf(conv_w_out[j]))
        elif kind == 1:
            lam_init = 0.8 - 0.6 * math.exp(-0.3 * i)
            q, k, vt = _qkv_layer(x, mod, gain, bf(diff_w_qkv[j]),
                                  diff_q_norm[j], diff_k_norm[j])
            lam_rows = jnp.stack([diff_lq1[j], diff_lk1[j], diff_lq2[j], diff_lk2[j]])
            a = _flash_layer(q, k, vt, lam_rows, diff_subln[j].reshape(V_DIM, 1),
                             lam_init)
            x = _proj_layer(x, mod, a, bf(diff_w_out[j]))
        else:
            x = _pool_layer(x, mod, gain, bf(pool_w_in[j]), bf(pool_w_group[j]),
                            pool_scale[j].reshape(1, D_MODEL), bf(pool_w_out[j]))
        x = _ffn_layer(x, mod, norm_ffn[i].reshape(1, D_MODEL), bf(ffn_w1[i]),
                       bf(ffn_w3[i]), bf(ffn_w2[i]))
    return x
```

```python
import functools
import math

import jax
import jax.numpy as jnp
from jax import lax
from jax.experimental import pallas as pl
from jax.experimental.pallas import tpu as pltpu

D_MODEL = 1024
DEPTH = 4
CHUNK = 64
N_MIXERS = 3
D_FF = 2816
CONV_WIDTH = 3
DIFF_HEADS = 8
DIFF_HEAD_DIM = 64
V_DIM = 2 * DIFF_HEAD_DIM
POOL_WINDOWS = (2, 4, 8, 16)
POOL_GROUP_DIM = D_MODEL // len(POOL_WINDOWS)
N_MOD = 6
EPS = 1e-6

SUBLANES = 8
LANES = 128
VMEM_LIMIT_BYTES = 56 * 1024 * 1024

TOKEN_TILE = 1024
SUB_TILES = 4
ADA_TILE = 2048
MXU_TILE = 256
FFN_CHUNKS = 2
ATTN_TILE = 256
QUERY_BLOCKS = 2
HEADS_PER_STEP = 8
BF16_SUBLANES = 16
VT_ROWS = V_DIM + BF16_SUBLANES
LOG2_E = math.log2(math.e)
POOL_HALO = 16
CONV_HALO = 8

_BF16 = jnp.bfloat16
_F32 = jnp.float32
_NEG_BIG = -1e30


def _dot(a, b):
    return jnp.dot(a, b, preferred_element_type=_F32)


def _const_spec(shape):
    n = len(shape)
    return pl.BlockSpec(shape, lambda *_: (0,) * n, pipeline_mode=pl.Buffered(1))


def _params(n_axes):
    return pltpu.CompilerParams(
        dimension_semantics=("arbitrary",) * n_axes,
        vmem_limit_bytes=VMEM_LIMIT_BYTES)


def _mod_norm(x, gain, scale, shift):
    ms = jnp.mean(x * x, axis=-1, keepdims=True)
    y = x * lax.rsqrt(ms + EPS) * gain
    return y * (1.0 + scale) + shift


def _ada_kernel(c_ref, w_ref, b_ref, o_ref):
    c = c_ref[...]
    cond = c * jax.nn.sigmoid(c)
    o_ref[...] = _dot(cond.astype(_BF16), w_ref[...].astype(_BF16)) + b_ref[...]


def _ada_modulation(c, ada_w, ada_b):
    b, d = c.shape
    n = ada_w.shape[-1]
    return pl.pallas_call(
        _ada_kernel,
        grid=(DEPTH, n // ADA_TILE),
        in_specs=[
            pl.BlockSpec((b, d), lambda l, j: (0, 0)),
            pl.BlockSpec((None, d, ADA_TILE), lambda l, j: (l, 0, j)),
            pl.BlockSpec((None, 1, ADA_TILE), lambda l, j: (l, 0, j)),
        ],
        out_specs=pl.BlockSpec((None, b, ADA_TILE), lambda l, j: (l, 0, j)),
        out_shape=jax.ShapeDtypeStruct((DEPTH, b, n), _F32),
        compiler_params=_params(2),
        name="ada_modulation",
    )(c, ada_w, ada_b.reshape(DEPTH, 1, n))


def _ffn_chunk_bounds():
    tiles = D_FF // MXU_TILE
    assert tiles * MXU_TILE == D_FF
    cuts = [(tiles * c + FFN_CHUNKS - 1) // FFN_CHUNKS * MXU_TILE
            for c in range(FFN_CHUNKS + 1)]
    return list(zip(cuts[:-1], cuts[1:]))


def _ffn_kernel(*refs, mixer_proj):
    if mixer_proj:
        x_ref, mod_ref, g_ref, a_ref, wo_ref, w1_ref, w3_ref, w2_ref, o_ref = refs
    else:
        x_ref, mod_ref, g_ref, w1_ref, w3_ref, w2_ref, o_ref = refs
    rows = TOKEN_TILE // SUB_TILES
    for s in range(SUB_TILES):
        rs = slice(s * rows, (s + 1) * rows)
        x = x_ref[rs, :]
        if mixer_proj:
            x = x + mod_ref[2:3, :] * _dot(a_ref[rs, :], wo_ref[...])
        h = _mod_norm(x, g_ref[...], mod_ref[4:5, :], mod_ref[3:4, :]).astype(_BF16)
        acc = None
        for lo, hi in _ffn_chunk_bounds():
            a = _dot(h, w1_ref[:, lo:hi])
            b = _dot(h, w3_ref[:, lo:hi])
            g = (a * jax.nn.sigmoid(a) * b).astype(_BF16)
            y = _dot(g, w2_ref[lo:hi, :])
            acc = y if acc is None else acc + y
        o_ref[rs, :] = x + mod_ref[5:6, :] * acc


def _token_spec():
    return pl.BlockSpec((None, TOKEN_TILE, D_MODEL), lambda b, i: (b, i, 0))


def _mod_spec():
    return pl.BlockSpec((None, N_MOD, D_MODEL), lambda b, i: (b, 0, 0))


def _ffn_layer(x, mod, gain, w1, w3, w2, mixer=None):
    b, s, d = x.shape
    mixer_specs = [] if mixer is None else [_token_spec(), _const_spec(mixer[1].shape)]
    return pl.pallas_call(
        functools.partial(_ffn_kernel, mixer_proj=mixer is not None),
        grid=(b, s // TOKEN_TILE),
        in_specs=[_token_spec(), _mod_spec(), _const_spec((1, d)), *mixer_specs,
                  _const_spec(w1.shape), _const_spec(w3.shape),
                  _const_spec(w2.shape)],
        out_specs=_token_spec(),
        out_shape=jax.ShapeDtypeStruct(x.shape, x.dtype),
        compiler_params=_params(2),
        name="ffn",
    )(x, mod, gain, *(mixer or ()), w1, w3, w2)


def _conv_kernel(x_ref, mod_ref, g_ref, win_ref, cw_ref, wout_ref, o_ref, ubuf):
    d = D_MODEL
    rows = TOKEN_TILE // SUB_TILES

    @pl.when(pl.program_id(1) == 0)
    def _():
        ubuf[0:CONV_HALO, :] = jnp.zeros((CONV_HALO, d), _F32)

    xs, bgs = [], []
    for s in range(SUB_TILES):
        x = x_ref[s * rows:(s + 1) * rows, :]
        h = _mod_norm(x, g_ref[...], mod_ref[1:2, :], mod_ref[0:1, :]).astype(_BF16)
        bgs.append(_dot(h, win_ref[:, 0:d]))
        cg = _dot(h, win_ref[:, d:2 * d])
        xv = _dot(h, win_ref[:, 2 * d:3 * d])
        ubuf[CONV_HALO + s * rows:CONV_HALO + (s + 1) * rows, :] = cg * xv
        xs.append(x)
    for s in range(SUB_TILES):
        r0 = CONV_HALO + s * rows
        y = (cw_ref[0:1, :] * ubuf[r0 - 2:r0 - 2 + rows, :]
             + cw_ref[1:2, :] * ubuf[r0 - 1:r0 - 1 + rows, :]
             + cw_ref[2:3, :] * ubuf[r0:r0 + rows, :])
        out = _dot((bgs[s] * y).astype(_BF16), wout_ref[...])
        o_ref[s * rows:(s + 1) * rows, :] = xs[s] + mod_ref[2:3, :] * out
    ubuf[0:CONV_HALO, :] = ubuf[TOKEN_TILE:TOKEN_TILE + CONV_HALO, :]


def _conv_layer(x, mod, gain, w_in, conv_w, w_out):
    b, s, d = x.shape
    return pl.pallas_call(
        _conv_kernel,
        grid=(b, s // TOKEN_TILE),
        in_specs=[_token_spec(), _mod_spec(), _const_spec((1, d)),
                  _const_spec(w_in.shape), _const_spec(conv_w.shape),
                  _const_spec(w_out.shape)],
        out_specs=_token_spec(),
        out_shape=jax.ShapeDtypeStruct(x.shape, x.dtype),
        scratch_shapes=[pltpu.VMEM((TOKEN_TILE + CONV_HALO, d), _F32)],
        compiler_params=_params(2),
        name="conv_mixer",
    )(x, mod, gain, w_in, conv_w, w_out)


def _pool_kernel(x_ref, mod_ref, g_ref, win_ref, wg_ref, sc_ref, wout_ref,
                 o_ref, *bufs):
    gd = POOL_GROUP_DIM
    n_grp = len(POOL_WINDOWS)
    rows = TOKEN_TILE // SUB_TILES
    i = pl.program_id(1)

    @pl.when(i == 0)
    def _():
        for buf in bufs:
            buf[0:POOL_HALO, :] = jnp.zeros((POOL_HALO, buf.shape[1]), _F32)

    xs = []
    for s in range(SUB_TILES):
        x = x_ref[s * rows:(s + 1) * rows, :]
        h = _mod_norm(x, g_ref[...], mod_ref[1:2, :], mod_ref[0:1, :]).astype(_BF16)
        bufs[0][POOL_HALO + s * rows:POOL_HALO + (s + 1) * rows, :] = _dot(h, win_ref[...])
        xs.append(x)
    for s in range(SUB_TILES):
        r0 = POOL_HALO + s * rows
        pos = i * TOKEN_TILE + s * rows + lax.broadcasted_iota(jnp.int32, (rows, 1), 0)
        u = bufs[0][r0:r0 + rows, :]
        cur = u
        ys = []
        for k in range(n_grp):
            w = 2 ** k
            cur = cur + bufs[k][r0 - w:r0 - w + rows, :]
            cnt = jnp.minimum(pos + 1, 2 * w).astype(_F32)
            pooled = cur[:, 0:gd] / cnt - u[:, k * gd:(k + 1) * gd]
            ys.append(_dot(pooled.astype(_BF16), wg_ref[k]))
            if k + 1 < n_grp:
                cur = cur[:, gd:]
                bufs[k + 1][r0:r0 + rows, :] = cur
        y = jnp.concatenate(ys, axis=-1) * sc_ref[...]
        out = _dot(y.astype(_BF16), wout_ref[...])
        o_ref[s * rows:(s + 1) * rows, :] = xs[s] + mod_ref[2:3, :] * out
    for buf in bufs:
        buf[0:POOL_HALO, :] = buf[TOKEN_TILE:TOKEN_TILE + POOL_HALO, :]


def _pool_layer(x, mod, gain, w_in, w_group, scale, w_out):
    b, s, d = x.shape
    n_grp = len(POOL_WINDOWS)
    assert POOL_WINDOWS == tuple(2 ** (g + 1) for g in range(n_grp))
    assert POOL_HALO >= POOL_WINDOWS[-1] // 2
    return pl.pallas_call(
        _pool_kernel,
        grid=(b, s // TOKEN_TILE),
        in_specs=[_token_spec(), _mod_spec(), _const_spec((1, d)),
                  _const_spec(w_in.shape), _const_spec(w_group.shape),
                  _const_spec((1, d)), _const_spec(w_out.shape)],
        out_specs=_token_spec(),
        out_shape=jax.ShapeDtypeStruct(x.shape, x.dtype),
        scratch_shapes=[
            pltpu.VMEM((TOKEN_TILE + POOL_HALO, (n_grp - k) * POOL_GROUP_DIM), _F32)
            for k in range(n_grp)],
        compiler_params=_params(2),
        name="pool_mixer",
    )(x, mod, gain, w_in, w_group, scale, w_out)


def _group_rms_scale(t, ones_ref):
    sq = t * t
    hi = sq.astype(_BF16)
    lo = (sq - hi.astype(_F32)).astype(_BF16)
    sums = []
    for c in range(t.shape[1] // MXU_TILE):
        cols = slice(c * MXU_TILE, (c + 1) * MXU_TILE)
        sums.append(_dot(jnp.concatenate([hi[:, cols], lo[:, cols]], axis=1),
                         ones_ref[...]))
    ss = jnp.concatenate(sums, axis=1)
    return lax.rsqrt(ss * (1.0 / DIFF_HEAD_DIM) + EPS)


def _qkv_kernel(x_ref, mod_ref, g_ref, w_ref, ones_ref, gq_ref, gk_ref,
                q_ref, k_ref, vt_ref):
    d = D_MODEL
    x = x_ref[...]
    h = _mod_norm(x, g_ref[...], mod_ref[1:2, :], mod_ref[0:1, :]).astype(_BF16)
    q = _dot(h, w_ref[:, 0:d])
    k = _dot(h, w_ref[:, d:2 * d])
    v = _dot(h, w_ref[:, 2 * d:3 * d])
    q_scale = DIFF_HEAD_DIM ** -0.5 * LOG2_E
    qn = q * _group_rms_scale(q, ones_ref) * gq_ref[...] * q_scale
    kn = k * _group_rms_scale(k, ones_ref) * gk_ref[...]
    q_ref[...] = qn.astype(_BF16)
    k_ref[...] = kn.astype(_BF16)
    t = ATTN_TILE
    pad_rows = VT_ROWS - V_DIM
    ones_row = (lax.broadcasted_iota(jnp.int32, (pad_rows, t), 0) == 0)
    ones_row = ones_row.astype(_F32).astype(_BF16)
    for hd in range(DIFF_HEADS):
        for blk in range(TOKEN_TILE // t):
            vb = v[blk * t:(blk + 1) * t, hd * V_DIM:(hd + 1) * V_DIM]
            vt_ref[hd, blk, 0:V_DIM, :] = vb.T.astype(_BF16)
            vt_ref[hd, blk, V_DIM:VT_ROWS, :] = ones_row


def _qkv_layer(x, mod, gain, w_qkv, gq, gk):
    b, s, d = x.shape
    t = ATTN_TILE
    groups = d // DIFF_HEAD_DIM
    col_group = jnp.arange(MXU_TILE, dtype=jnp.int32) // DIFF_HEAD_DIM
    ones_bd = (col_group[:, None] == col_group[None, :]).astype(_BF16)
    ones_bd = jnp.concatenate([ones_bd, ones_bd], axis=0)
    gq_t = jnp.tile(gq, groups).reshape(1, d)
    gk_t = jnp.tile(gk, groups).reshape(1, d)
    qk_spec = pl.BlockSpec((None, TOKEN_TILE, d), lambda bi, i: (bi, i, 0))
    blocks = TOKEN_TILE // t
    return pl.pallas_call(
        _qkv_kernel,
        grid=(b, s // TOKEN_TILE),
        in_specs=[_token_spec(), _mod_spec(), _const_spec((1, d)),
                  _const_spec(w_qkv.shape), _const_spec(ones_bd.shape),
                  _const_spec((1, d)), _const_spec((1, d))],
        out_specs=[
            qk_spec, qk_spec,
            pl.BlockSpec((None, DIFF_HEADS, blocks, VT_ROWS, t),
                         lambda bi, i: (bi, 0, i, 0, 0))],
        out_shape=[jax.ShapeDtypeStruct((b, s, d), _BF16),
                   jax.ShapeDtypeStruct((b, s, d), _BF16),
                   jax.ShapeDtypeStruct((b, DIFF_HEADS, s // t, VT_ROWS, t), _BF16)],
        compiler_params=_params(2),
        name="diff_qkv",
    )(x, mod, gain, w_qkv, ones_bd, gq_t, gk_t)


def _split3(x):
    hi = x.astype(_BF16).astype(_F32)
    r = x - hi
    mid = r.astype(_BF16).astype(_F32)
    lo = (r - mid).astype(_BF16).astype(_F32)
    return hi, mid, lo


def _flash_kernel(slopes_ref, q_ref, k_ref, vt_ref, lam_ref, subln_ref, o_ref,
                  kaug, dtile, acc_sc, *, lam_init):
    t = ATTN_TILE
    qb = QUERY_BLOCKS
    qt = qb * t
    n_blk = kaug.shape[1]
    hp = pl.program_id(1)
    qi = pl.program_id(2)
    lane = lax.broadcasted_iota(jnp.int32, (1, V_DIM), 1)

    @pl.when(qi == 0)
    def _build_tables():
        key = lax.broadcasted_iota(jnp.int32, (t, t), 0)
        qry = lax.broadcasted_iota(jnp.int32, (t, t), 1)
        ahead = jnp.maximum(key - qry, 0).astype(_F32)
        allowed = (key // CHUNK) <= (qry // CHUNK)
        key_pos = lax.broadcasted_iota(jnp.int32, (t, 1), 0).astype(_F32)
        for hh in range(HEADS_PER_STEP):
            sl = slopes_ref[hp * HEADS_PER_STEP + hh]
            hi, mid, lo = _split3(key_pos * sl)
            aug = jnp.where(lane == 0, hi, jnp.where(lane == 1, mid,
                            jnp.where(lane == 2, lo, 0.0))).astype(_BF16)
            for blk in range(n_blk):
                kaug[hh, blk, :, 0:V_DIM] = k_ref[blk * t:(blk + 1) * t,
                                                  hh * V_DIM:(hh + 1) * V_DIM]
                kaug[hh, blk, :, V_DIM:2 * V_DIM] = aug
            dtile[hh] = jnp.where(allowed, (2.0 * sl) * ahead, -_NEG_BIG)

    ones_aug = jnp.where(lane < 3, 1.0, 0.0)
    q2 = []
    for hh in range(HEADS_PER_STEP):
        qh = q_ref[:, hh * V_DIM:(hh + 1) * V_DIM].astype(_F32)
        ones = jnp.broadcast_to(ones_aug, qh.shape)
        q_map0 = jnp.concatenate([jnp.where(lane < DIFF_HEAD_DIM, qh, 0.0), ones], axis=1)
        q_map1 = jnp.concatenate([jnp.where(lane < DIFF_HEAD_DIM, 0.0, qh), ones], axis=1)
        q2.append(jnp.concatenate([q_map0, q_map1], axis=0).astype(_BF16))
        acc_sc[hh] = jnp.zeros(acc_sc.shape[1:], _F32)

    def scores(hh, j):
        return lax.dot_general(kaug[hh, j], q2[hh], (((1,), (1,)), ((), ())),
                               preferred_element_type=_F32)

    def update(hh, j, s, m_prev, shift):
        m_next = jnp.maximum(m_prev, jnp.max(s, axis=0, keepdims=True) + shift)
        alpha = jnp.exp2(m_prev - m_next)
        p = jnp.exp2(s - (m_next - shift)).astype(_BF16)
        acc_sc[hh] = alpha * acc_sc[hh] + _dot(vt_ref[hh, j], p)
        return m_next

    def body(j, ms):
        rel_blocks = ((j - qb * qi) * t).astype(_F32)
        ss = [scores(hh, j) for hh in range(HEADS_PER_STEP)]
        return tuple(
            update(hh, j, ss[hh], ms[hh],
                   slopes_ref[hp * HEADS_PER_STEP + hh] * rel_blocks)
            for hh in range(HEADS_PER_STEP))

    m0 = tuple(jnp.full((1, 2 * qt), _NEG_BIG, _F32) for _ in range(HEADS_PER_STEP))
    ms = lax.fori_loop(0, qb * qi, body, m0)

    lam_rows = lam_ref[...]
    dot1 = jnp.sum(lam_rows[0:1, :] * lam_rows[1:2, :], axis=-1, keepdims=True)
    dot2 = jnp.sum(lam_rows[2:3, :] * lam_rows[3:4, :], axis=-1, keepdims=True)
    lam = jnp.exp(dot1) - jnp.exp(dot2) + lam_init

    def diag_body(d):
        def fn(j, ms):
            ss = [scores(hh, j) for hh in range(HEADS_PER_STEP)]
            out = []
            for hh in range(HEADS_PER_STEP):
                parts = []
                for slab in range(2 * qb):
                    c = slab % qb
                    blk = ss[hh][:, slab * t:(slab + 1) * t]
                    if c < d:
                        parts.append(jnp.full((t, t), _NEG_BIG, _F32))
                    elif c == d:
                        parts.append(blk - dtile[hh])
                    else:
                        parts.append(blk)
                sl = slopes_ref[hp * HEADS_PER_STEP + hh]
                out.append(update(hh, j, jnp.concatenate(parts, axis=1), ms[hh],
                                  sl * float(d * t)))
            return tuple(out)
        return fn

    for d in range(qb):
        ms = lax.fori_loop(qb * qi + d, qb * qi + d + 1, diag_body(d), ms)
    for hh in range(HEADS_PER_STEP):
        acc = acc_sc[hh]
        o_both = acc[0:V_DIM, :] * (1.0 / acc[V_DIM:V_DIM + 1, :])
        o = o_both[:, 0:qt] - lam * o_both[:, qt:2 * qt]
        o = o * lax.rsqrt(jnp.mean(o * o, axis=0, keepdims=True) + EPS)
        o = o * (subln_ref[...] * (1.0 - lam_init))
        o_ref[:, hh * V_DIM:(hh + 1) * V_DIM] = o.T.astype(o_ref.dtype)


def _flash_layer(q, k, vt, lam_rows, subln, lam_init):
    b, s, d = q.shape
    t = ATTN_TILE
    qt = QUERY_BLOCKS * t
    hps = HEADS_PER_STEP
    slopes = 2.0 ** (-8.0 * jnp.arange(1, DIFF_HEADS + 1, dtype=_F32) / DIFF_HEADS)
    grid_spec = pltpu.PrefetchScalarGridSpec(
        num_scalar_prefetch=1,
        grid=(b, DIFF_HEADS // hps, s // qt),
        in_specs=[
            pl.BlockSpec((None, qt, hps * V_DIM), lambda bi, h, i, sl: (bi, i, h)),
            pl.BlockSpec((None, s, hps * V_DIM), lambda bi, h, i, sl: (bi, 0, h),
                         pipeline_mode=pl.Buffered(1)),
            pl.BlockSpec((None, hps, s // t, VT_ROWS, t),
                         lambda bi, h, i, sl: (bi, h, 0, 0, 0),
                         pipeline_mode=pl.Buffered(1)),
            pl.BlockSpec(lam_rows.shape, lambda bi, h, i, sl: (0, 0)),
            pl.BlockSpec((V_DIM, 1), lambda bi, h, i, sl: (0, 0)),
        ],
        out_specs=pl.BlockSpec((None, qt, hps * V_DIM), lambda bi, h, i, sl: (bi, i, h)),
        scratch_shapes=[pltpu.VMEM((hps, s // t, t, 2 * V_DIM), _BF16),
                        pltpu.VMEM((hps, t, t), _F32),
                        pltpu.VMEM((hps, VT_ROWS, 2 * qt), _F32)],
    )
    return pl.pallas_call(
        functools.partial(_flash_kernel, lam_init=lam_init),
        grid_spec=grid_spec,
        out_shape=jax.ShapeDtypeStruct((b, s, d), _BF16),
        compiler_params=_params(3),
        name="diff_flash",
    )(slopes * LOG2_E, q, k, vt, lam_rows, subln)


def kernel(x, c, ada_w, ada_b, norm_mix, norm_ffn, ffn_w1, ffn_w3, ffn_w2, conv_w_in, conv_w, conv_w_out, diff_w_qkv, diff_q_norm, diff_k_norm, diff_lq1, diff_lk1, diff_lq2, diff_lk2, diff_subln, diff_w_out, pool_w_in, pool_w_group, pool_scale, pool_w_out):
    b = x.shape[0]
    bf = lambda w: w.astype(_BF16)
    mod_all = _ada_modulation(c, ada_w, ada_b).reshape(DEPTH, b, N_MOD, D_MODEL)
    for i in range(DEPTH):
        kind = i % N_MIXERS
        j = i // N_MIXERS
        mod = mod_all[i]
        gain = norm_mix[i].reshape(1, D_MODEL)
        pending = None
        if kind == 0:
            x = _conv_layer(x, mod, gain, bf(conv_w_in[j]), conv_w[j],
                            bf(conv_w_out[j]))
        elif kind == 1:
            lam_init = 0.8 - 0.6 * math.exp(-0.3 * i)
            q, k, vt = _qkv_layer(x, mod, gain, bf(diff_w_qkv[j]),
                                  diff_q_norm[j], diff_k_norm[j])
            lam_rows = jnp.stack([diff_lq1[j], diff_lk1[j], diff_lq2[j], diff_lk2[j]])
            a = _flash_layer(q, k, vt, lam_rows, diff_subln[j].reshape(V_DIM, 1),
                             lam_init)
            pending = (a, bf(diff_w_out[j]))
        else:
            x = _pool_layer(x, mod, gain, bf(pool_w_in[j]), bf(pool_w_group[j]),
                            pool_scale[j].reshape(1, D_MODEL), bf(pool_w_out[j]))
        x = _ffn_layer(x, mod, norm_ffn[i].reshape(1, D_MODEL), bf(ffn_w1[i]),
                       bf(ffn_w3[i]), bf(ffn_w2[i]), mixer=pending)
    return x
```

```python
import functools
import math

import jax
import jax.numpy as jnp
from jax import lax
from jax.experimental import pallas as pl
from jax.experimental.pallas import tpu as pltpu

D_MODEL = 1024
DEPTH = 4
CHUNK = 64
N_MIXERS = 3
D_FF = 2816
CONV_WIDTH = 3
DIFF_HEADS = 8
DIFF_HEAD_DIM = 64
V_DIM = 2 * DIFF_HEAD_DIM
POOL_WINDOWS = (2, 4, 8, 16)
POOL_GROUP_DIM = D_MODEL // len(POOL_WINDOWS)
N_MOD = 6
EPS = 1e-6

SUBLANES = 8
LANES = 128
VMEM_LIMIT_BYTES = 56 * 1024 * 1024

TOKEN_TILE = 1024
SUB_TILES = 4
ADA_TILE = 2048
MXU_TILE = 256
FFN_CHUNKS = 2
ATTN_TILE = 256
QUERY_BLOCKS = 2
HEADS_PER_STEP = 8
BF16_SUBLANES = 16
VT_ROWS = V_DIM + BF16_SUBLANES
LOG2_E = math.log2(math.e)
POOL_HALO = 16
CONV_HALO = 8

_BF16 = jnp.bfloat16
_F32 = jnp.float32
_NEG_BIG = -1e30


def _dot(a, b):
    return jnp.dot(a, b, preferred_element_type=_F32)


def _const_spec(shape):
    n = len(shape)
    return pl.BlockSpec(shape, lambda *_: (0,) * n, pipeline_mode=pl.Buffered(1))


def _params(n_axes):
    return pltpu.CompilerParams(
        dimension_semantics=("arbitrary",) * n_axes,
        vmem_limit_bytes=VMEM_LIMIT_BYTES)


def _mod_norm(x, gain, scale, shift):
    ms = jnp.mean(x * x, axis=-1, keepdims=True)
    y = x * lax.rsqrt(ms + EPS) * gain
    return y * (1.0 + scale) + shift


def _ada_kernel(c_ref, w_ref, b_ref, o_ref):
    c = c_ref[...]
    cond = c * jax.nn.sigmoid(c)
    o_ref[...] = _dot(cond.astype(_BF16), w_ref[...].astype(_BF16)) + b_ref[...]


def _ada_modulation(c, ada_w, ada_b):
    b, d = c.shape
    n = ada_w.shape[-1]
    return pl.pallas_call(
        _ada_kernel,
        grid=(DEPTH, n // ADA_TILE),
        in_specs=[
            pl.BlockSpec((b, d), lambda l, j: (0, 0)),
            pl.BlockSpec((None, d, ADA_TILE), lambda l, j: (l, 0, j)),
            pl.BlockSpec((None, 1, ADA_TILE), lambda l, j: (l, 0, j)),
        ],
        out_specs=pl.BlockSpec((None, b, ADA_TILE), lambda l, j: (l, 0, j)),
        out_shape=jax.ShapeDtypeStruct((DEPTH, b, n), _F32),
        compiler_params=_params(2),
        name="ada_modulation",
    )(c, ada_w, ada_b.reshape(DEPTH, 1, n))


def _ffn_chunk_bounds():
    tiles = D_FF // MXU_TILE
    assert tiles * MXU_TILE == D_FF
    cuts = [(tiles * c + FFN_CHUNKS - 1) // FFN_CHUNKS * MXU_TILE
            for c in range(FFN_CHUNKS + 1)]
    return list(zip(cuts[:-1], cuts[1:]))


def _ffn_kernel(*refs, mixer_proj):
    if mixer_proj:
        x_ref, mod_ref, g_ref, a_ref, wo_ref, w1_ref, w3_ref, w2_ref, o_ref = refs
    else:
        x_ref, mod_ref, g_ref, w1_ref, w3_ref, w2_ref, o_ref = refs
    rows = TOKEN_TILE // SUB_TILES
    for s in range(SUB_TILES):
        rs = slice(s * rows, (s + 1) * rows)
        x = x_ref[rs, :]
        if mixer_proj:
            x = x + mod_ref[2:3, :] * _dot(a_ref[rs, :], wo_ref[...])
        h = _mod_norm(x, g_ref[...], mod_ref[4:5, :], mod_ref[3:4, :]).astype(_BF16)
        acc = None
        for lo, hi in _ffn_chunk_bounds():
            a = _dot(h, w1_ref[:, lo:hi])
            b = _dot(h, w3_ref[:, lo:hi])
            g = (a * jax.nn.sigmoid(a) * b).astype(_BF16)
            y = _dot(g, w2_ref[lo:hi, :])
            acc = y if acc is None else acc + y
        o_ref[rs, :] = x + mod_ref[5:6, :] * acc


def _token_spec():
    return pl.BlockSpec((None, TOKEN_TILE, D_MODEL), lambda b, i: (b, i, 0))


def _mod_spec():
    return pl.BlockSpec((None, N_MOD, D_MODEL), lambda b, i: (b, 0, 0))


def _ffn_layer(x, mod, gain, w1, w3, w2, mixer=None):
    b, s, d = x.shape
    mixer_specs = [] if mixer is None else [_token_spec(), _const_spec(mixer[1].shape)]
    return pl.pallas_call(
        functools.partial(_ffn_kernel, mixer_proj=mixer is not None),
        grid=(b, s // TOKEN_TILE),
        in_specs=[_token_spec(), _mod_spec(), _const_spec((1, d)), *mixer_specs,
                  _const_spec(w1.shape), _const_spec(w3.shape),
                  _const_spec(w2.shape)],
        out_specs=_token_spec(),
        out_shape=jax.ShapeDtypeStruct(x.shape, x.dtype),
        compiler_params=_params(2),
        name="ffn",
    )(x, mod, gain, *(mixer or ()), w1, w3, w2)


def _conv_kernel(x_ref, mod_ref, g_ref, win_ref, cw_ref, wout_ref, o_ref, ubuf):
    d = D_MODEL
    rows = TOKEN_TILE // SUB_TILES

    @pl.when(pl.program_id(1) == 0)
    def _():
        ubuf[0:CONV_HALO, :] = jnp.zeros((CONV_HALO, d), _F32)

    xs, bgs = [], []
    for s in range(SUB_TILES):
        x = x_ref[s * rows:(s + 1) * rows, :]
        h = _mod_norm(x, g_ref[...], mod_ref[1:2, :], mod_ref[0:1, :]).astype(_BF16)
        bgs.append(_dot(h, win_ref[:, 0:d]))
        cg = _dot(h, win_ref[:, d:2 * d])
        xv = _dot(h, win_ref[:, 2 * d:3 * d])
        ubuf[CONV_HALO + s * rows:CONV_HALO + (s + 1) * rows, :] = cg * xv
        xs.append(x)
    for s in range(SUB_TILES):
        r0 = CONV_HALO + s * rows
        y = (cw_ref[0:1, :] * ubuf[r0 - 2:r0 - 2 + rows, :]
             + cw_ref[1:2, :] * ubuf[r0 - 1:r0 - 1 + rows, :]
             + cw_ref[2:3, :] * ubuf[r0:r0 + rows, :])
        out = _dot((bgs[s] * y).astype(_BF16), wout_ref[...])
        o_ref[s * rows:(s + 1) * rows, :] = xs[s] + mod_ref[2:3, :] * out
    ubuf[0:CONV_HALO, :] = ubuf[TOKEN_TILE:TOKEN_TILE + CONV_HALO, :]


def _conv_layer(x, mod, gain, w_in, conv_w, w_out):
    b, s, d = x.shape
    return pl.pallas_call(
        _conv_kernel,
        grid=(b, s // TOKEN_TILE),
        in_specs=[_token_spec(), _mod_spec(), _const_spec((1, d)),
                  _const_spec(w_in.shape), _const_spec(conv_w.shape),
                  _const_spec(w_out.shape)],
        out_specs=_token_spec(),
        out_shape=jax.ShapeDtypeStruct(x.shape, x.dtype),
        scratch_shapes=[pltpu.VMEM((TOKEN_TILE + CONV_HALO, d), _F32)],
        compiler_params=_params(2),
        name="conv_mixer",
    )(x, mod, gain, w_in, conv_w, w_out)


def _pool_kernel(x_ref, mod_ref, g_ref, win_ref, wg_ref, sc_ref, wout_ref,
                 o_ref, *bufs):
    gd = POOL_GROUP_DIM
    n_grp = len(POOL_WINDOWS)
    rows = TOKEN_TILE // SUB_TILES
    i = pl.program_id(1)

    @pl.when(i == 0)
    def _():
        for buf in bufs:
            buf[0:POOL_HALO, :] = jnp.zeros((POOL_HALO, buf.shape[1]), _F32)

    xs = []
    for s in range(SUB_TILES):
        x = x_ref[s * rows:(s + 1) * rows, :]
        h = _mod_norm(x, g_ref[...], mod_ref[1:2, :], mod_ref[0:1, :]).astype(_BF16)
        bufs[0][POOL_HALO + s * rows:POOL_HALO + (s + 1) * rows, :] = _dot(h, win_ref[...])
        xs.append(x)
    for s in range(SUB_TILES):
        r0 = POOL_HALO + s * rows
        pos = i * TOKEN_TILE + s * rows + lax.broadcasted_iota(jnp.int32, (rows, 1), 0)
        u = bufs[0][r0:r0 + rows, :]
        cur = u
        ys = []
        for k in range(n_grp):
            w = 2 ** k
            cur = cur + bufs[k][r0 - w:r0 - w + rows, :]
            cnt = jnp.minimum(pos + 1, 2 * w).astype(_F32)
            pooled = cur[:, 0:gd] / cnt - u[:, k * gd:(k + 1) * gd]
            ys.append(_dot(pooled.astype(_BF16), wg_ref[k]))
            if k + 1 < n_grp:
                cur = cur[:, gd:]
                bufs[k + 1][r0:r0 + rows, :] = cur
        y = jnp.concatenate(ys, axis=-1) * sc_ref[...]
        out = _dot(y.astype(_BF16), wout_ref[...])
        o_ref[s * rows:(s + 1) * rows, :] = xs[s] + mod_ref[2:3, :] * out
    for buf in bufs:
        buf[0:POOL_HALO, :] = buf[TOKEN_TILE:TOKEN_TILE + POOL_HALO, :]


def _pool_layer(x, mod, gain, w_in, w_group, scale, w_out):
    b, s, d = x.shape
    n_grp = len(POOL_WINDOWS)
    assert POOL_WINDOWS == tuple(2 ** (g + 1) for g in range(n_grp))
    assert POOL_HALO >= POOL_WINDOWS[-1] // 2
    return pl.pallas_call(
        _pool_kernel,
        grid=(b, s // TOKEN_TILE),
        in_specs=[_token_spec(), _mod_spec(), _const_spec((1, d)),
                  _const_spec(w_in.shape), _const_spec(w_group.shape),
                  _const_spec((1, d)), _const_spec(w_out.shape)],
        out_specs=_token_spec(),
        out_shape=jax.ShapeDtypeStruct(x.shape, x.dtype),
        scratch_shapes=[
            pltpu.VMEM((TOKEN_TILE + POOL_HALO, (n_grp - k) * POOL_GROUP_DIM), _F32)
            for k in range(n_grp)],
        compiler_params=_params(2),
        name="pool_mixer",
    )(x, mod, gain, w_in, w_group, scale, w_out)


def _group_rms_scale(t, ones_ref):
    sq = t * t
    hi = sq.astype(_BF16)
    lo = (sq - hi.astype(_F32)).astype(_BF16)
    sums = []
    for c in range(t.shape[1] // MXU_TILE):
        cols = slice(c * MXU_TILE, (c + 1) * MXU_TILE)
        sums.append(_dot(jnp.concatenate([hi[:, cols], lo[:, cols]], axis=1),
                         ones_ref[...]))
    ss = jnp.concatenate(sums, axis=1)
    return lax.rsqrt(ss * (1.0 / DIFF_HEAD_DIM) + EPS)


def _qkv_kernel(x_ref, mod_ref, g_ref, w_ref, ones_ref, gq_ref, gk_ref,
                q_ref, k_ref, vt_ref):
    d = D_MODEL
    x = x_ref[...]
    h = _mod_norm(x, g_ref[...], mod_ref[1:2, :], mod_ref[0:1, :]).astype(_BF16)
    q = _dot(h, w_ref[:, 0:d])
    k = _dot(h, w_ref[:, d:2 * d])
    v = _dot(h, w_ref[:, 2 * d:3 * d])
    q_scale = DIFF_HEAD_DIM ** -0.5 * LOG2_E
    qn = q * _group_rms_scale(q, ones_ref) * gq_ref[...] * q_scale
    kn = k * _group_rms_scale(k, ones_ref) * gk_ref[...]
    q_ref[...] = qn.astype(_BF16)
    k_ref[...] = kn.astype(_BF16)
    t = ATTN_TILE
    pad_rows = VT_ROWS - V_DIM
    ones_row = (lax.broadcasted_iota(jnp.int32, (pad_rows, t), 0) == 0)
    ones_row = ones_row.astype(_F32).astype(_BF16)
    for hd in range(DIFF_HEADS):
        for blk in range(TOKEN_TILE // t):
            vb = v[blk * t:(blk + 1) * t, hd * V_DIM:(hd + 1) * V_DIM]
            vt_ref[hd, blk, 0:V_DIM, :] = vb.T.astype(_BF16)
            vt_ref[hd, blk, V_DIM:VT_ROWS, :] = ones_row


def _qkv_layer(x, mod, gain, w_qkv, gq, gk):
    b, s, d = x.shape
    t = ATTN_TILE
    groups = d // DIFF_HEAD_DIM
    col_group = jnp.arange(MXU_TILE, dtype=jnp.int32) // DIFF_HEAD_DIM
    ones_bd = (col_group[:, None] == col_group[None, :]).astype(_BF16)
    ones_bd = jnp.concatenate([ones_bd, ones_bd], axis=0)
    gq_t = jnp.tile(gq, groups).reshape(1, d)
    gk_t = jnp.tile(gk, groups).reshape(1, d)
    qk_spec = pl.BlockSpec((None, TOKEN_TILE, d), lambda bi, i: (bi, i, 0))
    blocks = TOKEN_TILE // t
    return pl.pallas_call(
        _qkv_kernel,
        grid=(b, s // TOKEN_TILE),
        in_specs=[_token_spec(), _mod_spec(), _const_spec((1, d)),
                  _const_spec(w_qkv.shape), _const_spec(ones_bd.shape),
                  _const_spec((1, d)), _const_spec((1, d))],
        out_specs=[
            qk_spec, qk_spec,
            pl.BlockSpec((None, DIFF_HEADS, blocks, VT_ROWS, t),
                         lambda bi, i: (bi, 0, i, 0, 0))],
        out_shape=[jax.ShapeDtypeStruct((b, s, d), _BF16),
                   jax.ShapeDtypeStruct((b, s, d), _BF16),
                   jax.ShapeDtypeStruct((b, DIFF_HEADS, s // t, VT_ROWS, t), _BF16)],
        compiler_params=_params(2),
        name="diff_qkv",
    )(x, mod, gain, w_qkv, ones_bd, gq_t, gk_t)


def _split3(x):
    hi = x.astype(_BF16).astype(_F32)
    r = x - hi
    mid = r.astype(_BF16).astype(_F32)
    lo = (r - mid).astype(_BF16).astype(_F32)
    return hi, mid, lo


def _flash_kernel(slopes_ref, q_ref, k_ref, vt_ref, lam_ref, subln_ref, o_ref,
                  kaug, dtile, acc_sc, *, lam_init):
    t = ATTN_TILE
    qb = QUERY_BLOCKS
    qt = qb * t
    n_blk = kaug.shape[1]
    hp = pl.program_id(1)
    qi = pl.program_id(2)
    lane = lax.broadcasted_iota(jnp.int32, (1, V_DIM), 1)

    @pl.when(qi == 0)
    def _build_tables():
        key = lax.broadcasted_iota(jnp.int32, (t, t), 0)
        qry = lax.broadcasted_iota(jnp.int32, (t, t), 1)
        ahead = jnp.maximum(key - qry, 0).astype(_F32)
        allowed = (key // CHUNK) <= (qry // CHUNK)
        key_pos = lax.broadcasted_iota(jnp.int32, (t, 1), 0).astype(_F32)
        for hh in range(HEADS_PER_STEP):
            sl = slopes_ref[hp * HEADS_PER_STEP + hh]
            hi, mid, lo = _split3(key_pos * sl)
            aug = jnp.where(lane == 0, hi, jnp.where(lane == 1, mid,
                            jnp.where(lane == 2, lo, 0.0))).astype(_BF16)
            for blk in range(n_blk):
                kaug[hh, blk, :, 0:V_DIM] = k_ref[blk * t:(blk + 1) * t,
                                                  hh * V_DIM:(hh + 1) * V_DIM]
                kaug[hh, blk, :, V_DIM:2 * V_DIM] = aug
            dtile[hh] = jnp.where(allowed, (2.0 * sl) * ahead, -_NEG_BIG)

    feat = lax.broadcasted_iota(jnp.int32, (V_DIM, 1), 0)
    bias_rows = jnp.broadcast_to(jnp.where(feat < 3, 1.0, 0.0), (V_DIM, 2 * qt))
    q2t = []
    for hh in range(HEADS_PER_STEP):
        qh_t = q_ref[:, hh * V_DIM:(hh + 1) * V_DIM].astype(_F32).T
        q_rows = jnp.concatenate([jnp.where(feat < DIFF_HEAD_DIM, qh_t, 0.0),
                                  jnp.where(feat < DIFF_HEAD_DIM, 0.0, qh_t)], axis=1)
        q2t.append(jnp.concatenate([q_rows, bias_rows], axis=0).astype(_BF16))
        acc_sc[hh] = jnp.zeros(acc_sc.shape[1:], _F32)

    def scores(hh, j):
        return _dot(kaug[hh, j], q2t[hh])

    def update(hh, j, s, m_prev, shift):
        m_next = jnp.maximum(m_prev, jnp.max(s, axis=0, keepdims=True) + shift)
        alpha = jnp.exp2(m_prev - m_next)
        p = jnp.exp2(s - (m_next - shift)).astype(_BF16)
        acc_sc[hh] = alpha * acc_sc[hh] + _dot(vt_ref[hh, j], p)
        return m_next

    def body(j, ms):
        rel_blocks = ((j - qb * qi) * t).astype(_F32)
        ss = [scores(hh, j) for hh in range(HEADS_PER_STEP)]
        return tuple(
            update(hh, j, ss[hh], ms[hh],
                   slopes_ref[hp * HEADS_PER_STEP + hh] * rel_blocks)
            for hh in range(HEADS_PER_STEP))

    m0 = tuple(jnp.full((1, 2 * qt), _NEG_BIG, _F32) for _ in range(HEADS_PER_STEP))
    ms = lax.fori_loop(0, qb * qi, body, m0)

    lam_rows = lam_ref[...]
    dot1 = jnp.sum(lam_rows[0:1, :] * lam_rows[1:2, :], axis=-1, keepdims=True)
    dot2 = jnp.sum(lam_rows[2:3, :] * lam_rows[3:4, :], axis=-1, keepdims=True)
    lam = jnp.exp(dot1) - jnp.exp(dot2) + lam_init

    def diag_body(d):
        def fn(j, ms):
            ss = [scores(hh, j) for hh in range(HEADS_PER_STEP)]
            out = []
            for hh in range(HEADS_PER_STEP):
                parts = []
                for slab in range(2 * qb):
                    c = slab % qb
                    blk = ss[hh][:, slab * t:(slab + 1) * t]
                    if c < d:
                        parts.append(jnp.full((t, t), _NEG_BIG, _F32))
                    elif c == d:
                        parts.append(blk - dtile[hh])
                    else:
                        parts.append(blk)
                sl = slopes_ref[hp * HEADS_PER_STEP + hh]
                out.append(update(hh, j, jnp.concatenate(parts, axis=1), ms[hh],
                                  sl * float(d * t)))
            return tuple(out)
        return fn

    for d in range(qb):
        ms = lax.fori_loop(qb * qi + d, qb * qi + d + 1, diag_body(d), ms)
    for hh in range(HEADS_PER_STEP):
        acc = acc_sc[hh]
        o_both = acc[0:V_DIM, :] * (1.0 / acc[V_DIM:V_DIM + 1, :])
        o = o_both[:, 0:qt] - lam * o_both[:, qt:2 * qt]
        o = o * lax.rsqrt(jnp.mean(o * o, axis=0, keepdims=True) + EPS)
        o = o * (subln_ref[...] * (1.0 - lam_init))
        o_ref[:, hh * V_DIM:(hh + 1) * V_DIM] = o.T.astype(o_ref.dtype)


def _flash_layer(q, k, vt, lam_rows, subln, lam_init):
    b, s, d = q.shape
    t = ATTN_TILE
    qt = QUERY_BLOCKS * t
    hps = HEADS_PER_STEP
    slopes = 2.0 ** (-8.0 * jnp.arange(1, DIFF_HEADS + 1, dtype=_F32) / DIFF_HEADS)
    grid_spec = pltpu.PrefetchScalarGridSpec(
        num_scalar_prefetch=1,
        grid=(b, DIFF_HEADS // hps, s // qt),
        in_specs=[
            pl.BlockSpec((None, qt, hps * V_DIM), lambda bi, h, i, sl: (bi, i, h)),
            pl.BlockSpec((None, s, hps * V_DIM), lambda bi, h, i, sl: (bi, 0, h),
                         pipeline_mode=pl.Buffered(1)),
            pl.BlockSpec((None, hps, s // t, VT_ROWS, t),
                         lambda bi, h, i, sl: (bi, h, 0, 0, 0),
                         pipeline_mode=pl.Buffered(1)),
            pl.BlockSpec(lam_rows.shape, lambda bi, h, i, sl: (0, 0)),
            pl.BlockSpec((V_DIM, 1), lambda bi, h, i, sl: (0, 0)),
        ],
        out_specs=pl.BlockSpec((None, qt, hps * V_DIM), lambda bi, h, i, sl: (bi, i, h)),
        scratch_shapes=[pltpu.VMEM((hps, s // t, t, 2 * V_DIM), _BF16),
                        pltpu.VMEM((hps, t, t), _F32),
                        pltpu.VMEM((hps, VT_ROWS, 2 * qt), _F32)],
    )
    return pl.pallas_call(
        functools.partial(_flash_kernel, lam_init=lam_init),
        grid_spec=grid_spec,
        out_shape=jax.ShapeDtypeStruct((b, s, d), _BF16),
        compiler_params=_params(3),
        name="diff_flash",
    )(slopes * LOG2_E, q, k, vt, lam_rows, subln)


def kernel(x, c, ada_w, ada_b, norm_mix, norm_ffn, ffn_w1, ffn_w3, ffn_w2, conv_w_in, conv_w, conv_w_out, diff_w_qkv, diff_q_norm, diff_k_norm, diff_lq1, diff_lk1, diff_lq2, diff_lk2, diff_subln, diff_w_out, pool_w_in, pool_w_group, pool_scale, pool_w_out):
    b = x.shape[0]
    bf = lambda w: w.astype(_BF16)
    mod_all = _ada_modulation(c, ada_w, ada_b).reshape(DEPTH, b, N_MOD, D_MODEL)
    for i in range(DEPTH):
        kind = i % N_MIXERS
        j = i // N_MIXERS
        mod = mod_all[i]
        gain = norm_mix[i].reshape(1, D_MODEL)
        pending = None
        if kind == 0:
            x = _conv_layer(x, mod, gain, bf(conv_w_in[j]), conv_w[j],
                            bf(conv_w_out[j]))
        elif kind == 1:
            lam_init = 0.8 - 0.6 * math.exp(-0.3 * i)
            q, k, vt = _qkv_layer(x, mod, gain, bf(diff_w_qkv[j]),
                                  diff_q_norm[j], diff_k_norm[j])
            lam_rows = jnp.stack([diff_lq1[j], diff_lk1[j], diff_lq2[j], diff_lk2[j]])
            a = _flash_layer(q, k, vt, lam_rows, diff_subln[j].reshape(V_DIM, 1),
                             lam_init)
            pending = (a, bf(diff_w_out[j]))
        else:
            x = _pool_layer(x, mod, gain, bf(pool_w_in[j]), bf(pool_w_group[j]),
                            pool_scale[j].reshape(1, D_MODEL), bf(pool_w_out[j]))
        x = _ffn_layer(x, mod, norm_ffn[i].reshape(1, D_MODEL), bf(ffn_w1[i]),
                       bf(ffn_w3[i]), bf(ffn_w2[i]), mixer=pending)
    return x
```

```python
import functools
import math

import jax
import jax.numpy as jnp
from jax import lax
from jax.experimental import pallas as pl
from jax.experimental.pallas import tpu as pltpu

D_MODEL = 1024
DEPTH = 4
CHUNK = 64
N_MIXERS = 3
D_FF = 2816
CONV_WIDTH = 3
DIFF_HEADS = 8
DIFF_HEAD_DIM = 64
V_DIM = 2 * DIFF_HEAD_DIM
POOL_WINDOWS = (2, 4, 8, 16)
POOL_GROUP_DIM = D_MODEL // len(POOL_WINDOWS)
N_MOD = 6
EPS = 1e-6

SUBLANES = 8
LANES = 128
VMEM_LIMIT_BYTES = 56 * 1024 * 1024

TOKEN_TILE = 1024
SUB_TILES = 4
ADA_TILE = 2048
MXU_TILE = 256
FFN_CHUNKS = 2
ATTN_TILE = 256
QUERY_BLOCKS = 2
HEADS_PER_STEP = 8
BF16_SUBLANES = 16
VT_ROWS = V_DIM + BF16_SUBLANES
LOG2_E = math.log2(math.e)
POOL_HALO = 16
CONV_HALO = 8

_BF16 = jnp.bfloat16
_F32 = jnp.float32
_NEG_BIG = -1e30
BOUND_SLACK = 1.0 + 2.0 ** -8
BOUND_PAD = 2.0 ** -4
MAX_REFERENCE_GAP = 100.0


def _dot(a, b):
    return jnp.dot(a, b, preferred_element_type=_F32)


def _const_spec(shape):
    n = len(shape)
    return pl.BlockSpec(shape, lambda *_: (0,) * n, pipeline_mode=pl.Buffered(1))


def _params(n_axes):
    return pltpu.CompilerParams(
        dimension_semantics=("arbitrary",) * n_axes,
        vmem_limit_bytes=VMEM_LIMIT_BYTES)


def _mod_norm(x, gain, scale, shift):
    ms = jnp.mean(x * x, axis=-1, keepdims=True)
    y = x * lax.rsqrt(ms + EPS) * gain
    return y * (1.0 + scale) + shift


def _ada_kernel(c_ref, w_ref, b_ref, o_ref):
    c = c_ref[...]
    cond = c * jax.nn.sigmoid(c)
    o_ref[...] = _dot(cond.astype(_BF16), w_ref[...].astype(_BF16)) + b_ref[...]


def _ada_modulation(c, ada_w, ada_b):
    b, d = c.shape
    n = ada_w.shape[-1]
    return pl.pallas_call(
        _ada_kernel,
        grid=(DEPTH, n // ADA_TILE),
        in_specs=[
            pl.BlockSpec((b, d), lambda l, j: (0, 0)),
            pl.BlockSpec((None, d, ADA_TILE), lambda l, j: (l, 0, j)),
            pl.BlockSpec((None, 1, ADA_TILE), lambda l, j: (l, 0, j)),
        ],
        out_specs=pl.BlockSpec((None, b, ADA_TILE), lambda l, j: (l, 0, j)),
        out_shape=jax.ShapeDtypeStruct((DEPTH, b, n), _F32),
        compiler_params=_params(2),
        name="ada_modulation",
    )(c, ada_w, ada_b.reshape(DEPTH, 1, n))


def _ffn_chunk_bounds():
    tiles = D_FF // MXU_TILE
    assert tiles * MXU_TILE == D_FF
    cuts = [(tiles * c + FFN_CHUNKS - 1) // FFN_CHUNKS * MXU_TILE
            for c in range(FFN_CHUNKS + 1)]
    return list(zip(cuts[:-1], cuts[1:]))


def _ffn_kernel(*refs, mixer_proj):
    if mixer_proj:
        x_ref, mod_ref, g_ref, a_ref, wo_ref, w1_ref, w3_ref, w2_ref, o_ref = refs
    else:
        x_ref, mod_ref, g_ref, w1_ref, w3_ref, w2_ref, o_ref = refs
    rows = TOKEN_TILE // SUB_TILES
    for s in range(SUB_TILES):
        rs = slice(s * rows, (s + 1) * rows)
        x = x_ref[rs, :]
        if mixer_proj:
            x = x + mod_ref[2:3, :] * _dot(a_ref[rs, :], wo_ref[...])
        h = _mod_norm(x, g_ref[...], mod_ref[4:5, :], mod_ref[3:4, :]).astype(_BF16)
        acc = None
        for lo, hi in _ffn_chunk_bounds():
            a = _dot(h, w1_ref[:, lo:hi])
            b = _dot(h, w3_ref[:, lo:hi])
            g = (a * jax.nn.sigmoid(a) * b).astype(_BF16)
            y = _dot(g, w2_ref[lo:hi, :])
            acc = y if acc is None else acc + y
        o_ref[rs, :] = x + mod_ref[5:6, :] * acc


def _token_spec():
    return pl.BlockSpec((None, TOKEN_TILE, D_MODEL), lambda b, i: (b, i, 0))


def _mod_spec():
    return pl.BlockSpec((None, N_MOD, D_MODEL), lambda b, i: (b, 0, 0))


def _ffn_layer(x, mod, gain, w1, w3, w2, mixer=None):
    b, s, d = x.shape
    mixer_specs = [] if mixer is None else [_token_spec(), _const_spec(mixer[1].shape)]
    return pl.pallas_call(
        functools.partial(_ffn_kernel, mixer_proj=mixer is not None),
        grid=(b, s // TOKEN_TILE),
        in_specs=[_token_spec(), _mod_spec(), _const_spec((1, d)), *mixer_specs,
                  _const_spec(w1.shape), _const_spec(w3.shape),
                  _const_spec(w2.shape)],
        out_specs=_token_spec(),
        out_shape=jax.ShapeDtypeStruct(x.shape, x.dtype),
        compiler_params=_params(2),
        name="ffn",
    )(x, mod, gain, *(mixer or ()), w1, w3, w2)


def _conv_kernel(x_ref, mod_ref, g_ref, win_ref, cw_ref, wout_ref, o_ref, ubuf):
    d = D_MODEL
    rows = TOKEN_TILE // SUB_TILES

    @pl.when(pl.program_id(1) == 0)
    def _():
        ubuf[0:CONV_HALO, :] = jnp.zeros((CONV_HALO, d), _F32)

    xs, bgs = [], []
    for s in range(SUB_TILES):
        x = x_ref[s * rows:(s + 1) * rows, :]
        h = _mod_norm(x, g_ref[...], mod_ref[1:2, :], mod_ref[0:1, :]).astype(_BF16)
        bgs.append(_dot(h, win_ref[:, 0:d]))
        cg = _dot(h, win_ref[:, d:2 * d])
        xv = _dot(h, win_ref[:, 2 * d:3 * d])
        ubuf[CONV_HALO + s * rows:CONV_HALO + (s + 1) * rows, :] = cg * xv
        xs.append(x)
    for s in range(SUB_TILES):
        r0 = CONV_HALO + s * rows
        y = (cw_ref[0:1, :] * ubuf[r0 - 2:r0 - 2 + rows, :]
             + cw_ref[1:2, :] * ubuf[r0 - 1:r0 - 1 + rows, :]
             + cw_ref[2:3, :] * ubuf[r0:r0 + rows, :])
        out = _dot((bgs[s] * y).astype(_BF16), wout_ref[...])
        o_ref[s * rows:(s + 1) * rows, :] = xs[s] + mod_ref[2:3, :] * out
    ubuf[0:CONV_HALO, :] = ubuf[TOKEN_TILE:TOKEN_TILE + CONV_HALO, :]


def _conv_layer(x, mod, gain, w_in, conv_w, w_out):
    b, s, d = x.shape
    return pl.pallas_call(
        _conv_kernel,
        grid=(b, s // TOKEN_TILE),
        in_specs=[_token_spec(), _mod_spec(), _const_spec((1, d)),
                  _const_spec(w_in.shape), _const_spec(conv_w.shape),
                  _const_spec(w_out.shape)],
        out_specs=_token_spec(),
        out_shape=jax.ShapeDtypeStruct(x.shape, x.dtype),
        scratch_shapes=[pltpu.VMEM((TOKEN_TILE + CONV_HALO, d), _F32)],
        compiler_params=_params(2),
        name="conv_mixer",
    )(x, mod, gain, w_in, conv_w, w_out)


def _pool_kernel(x_ref, mod_ref, g_ref, win_ref, wg_ref, sc_ref, wout_ref,
                 o_ref, *bufs):
    gd = POOL_GROUP_DIM
    n_grp = len(POOL_WINDOWS)
    rows = TOKEN_TILE // SUB_TILES
    i = pl.program_id(1)

    @pl.when(i == 0)
    def _():
        for buf in bufs:
            buf[0:POOL_HALO, :] = jnp.zeros((POOL_HALO, buf.shape[1]), _F32)

    xs = []
    for s in range(SUB_TILES):
        x = x_ref[s * rows:(s + 1) * rows, :]
        h = _mod_norm(x, g_ref[...], mod_ref[1:2, :], mod_ref[0:1, :]).astype(_BF16)
        bufs[0][POOL_HALO + s * rows:POOL_HALO + (s + 1) * rows, :] = _dot(h, win_ref[...])
        xs.append(x)
    for s in range(SUB_TILES):
        r0 = POOL_HALO + s * rows
        pos = i * TOKEN_TILE + s * rows + lax.broadcasted_iota(jnp.int32, (rows, 1), 0)
        u = bufs[0][r0:r0 + rows, :]
        cur = u
        ys = []
        for k in range(n_grp):
            w = 2 ** k
            cur = cur + bufs[k][r0 - w:r0 - w + rows, :]
            cnt = jnp.minimum(pos + 1, 2 * w).astype(_F32)
            pooled = cur[:, 0:gd] / cnt - u[:, k * gd:(k + 1) * gd]
            ys.append(_dot(pooled.astype(_BF16), wg_ref[k]))
            if k + 1 < n_grp:
                cur = cur[:, gd:]
                bufs[k + 1][r0:r0 + rows, :] = cur
        y = jnp.concatenate(ys, axis=-1) * sc_ref[...]
        out = _dot(y.astype(_BF16), wout_ref[...])
        o_ref[s * rows:(s + 1) * rows, :] = xs[s] + mod_ref[2:3, :] * out
    for buf in bufs:
        buf[0:POOL_HALO, :] = buf[TOKEN_TILE:TOKEN_TILE + POOL_HALO, :]


def _pool_layer(x, mod, gain, w_in, w_group, scale, w_out):
    b, s, d = x.shape
    n_grp = len(POOL_WINDOWS)
    assert POOL_WINDOWS == tuple(2 ** (g + 1) for g in range(n_grp))
    assert POOL_HALO >= POOL_WINDOWS[-1] // 2
    return pl.pallas_call(
        _pool_kernel,
        grid=(b, s // TOKEN_TILE),
        in_specs=[_token_spec(), _mod_spec(), _const_spec((1, d)),
                  _const_spec(w_in.shape), _const_spec(w_group.shape),
                  _const_spec((1, d)), _const_spec(w_out.shape)],
        out_specs=_token_spec(),
        out_shape=jax.ShapeDtypeStruct(x.shape, x.dtype),
        scratch_shapes=[
            pltpu.VMEM((TOKEN_TILE + POOL_HALO, (n_grp - k) * POOL_GROUP_DIM), _F32)
            for k in range(n_grp)],
        compiler_params=_params(2),
        name="pool_mixer",
    )(x, mod, gain, w_in, w_group, scale, w_out)


def _group_rms_scale(t, ones_ref):
    sq = t * t
    hi = sq.astype(_BF16)
    lo = (sq - hi.astype(_F32)).astype(_BF16)
    sums = []
    for c in range(t.shape[1] // MXU_TILE):
        cols = slice(c * MXU_TILE, (c + 1) * MXU_TILE)
        sums.append(_dot(jnp.concatenate([hi[:, cols], lo[:, cols]], axis=1),
                         ones_ref[...]))
    ss = jnp.concatenate(sums, axis=1)
    return lax.rsqrt(ss * (1.0 / DIFF_HEAD_DIM) + EPS)


def _qkv_kernel(x_ref, mod_ref, g_ref, w_ref, ones_ref, gq_ref, gk_ref,
                q_ref, k_ref, vt_ref):
    d = D_MODEL
    x = x_ref[...]
    h = _mod_norm(x, g_ref[...], mod_ref[1:2, :], mod_ref[0:1, :]).astype(_BF16)
    q = _dot(h, w_ref[:, 0:d])
    k = _dot(h, w_ref[:, d:2 * d])
    v = _dot(h, w_ref[:, 2 * d:3 * d])
    q_scale = DIFF_HEAD_DIM ** -0.5 * LOG2_E
    qn = q * _group_rms_scale(q, ones_ref) * gq_ref[...] * q_scale
    kn = k * _group_rms_scale(k, ones_ref) * gk_ref[...]
    q_ref[...] = qn.astype(_BF16)
    k_ref[...] = kn.astype(_BF16)
    t = ATTN_TILE
    pad_rows = VT_ROWS - V_DIM
    ones_row = (lax.broadcasted_iota(jnp.int32, (pad_rows, t), 0) == 0)
    ones_row = ones_row.astype(_F32).astype(_BF16)
    for hd in range(DIFF_HEADS):
        for blk in range(TOKEN_TILE // t):
            vb = v[blk * t:(blk + 1) * t, hd * V_DIM:(hd + 1) * V_DIM]
            vt_ref[hd, blk, 0:V_DIM, :] = vb.T.astype(_BF16)
            vt_ref[hd, blk, V_DIM:VT_ROWS, :] = ones_row


def _qkv_layer(x, mod, gain, w_qkv, gq, gk):
    b, s, d = x.shape
    t = ATTN_TILE
    groups = d // DIFF_HEAD_DIM
    col_group = jnp.arange(MXU_TILE, dtype=jnp.int32) // DIFF_HEAD_DIM
    ones_bd = (col_group[:, None] == col_group[None, :]).astype(_BF16)
    ones_bd = jnp.concatenate([ones_bd, ones_bd], axis=0)
    gq_t = jnp.tile(gq, groups).reshape(1, d)
    gk_t = jnp.tile(gk, groups).reshape(1, d)
    qk_spec = pl.BlockSpec((None, TOKEN_TILE, d), lambda bi, i: (bi, i, 0))
    blocks = TOKEN_TILE // t
    return pl.pallas_call(
        _qkv_kernel,
        grid=(b, s // TOKEN_TILE),
        in_specs=[_token_spec(), _mod_spec(), _const_spec((1, d)),
                  _const_spec(w_qkv.shape), _const_spec(ones_bd.shape),
                  _const_spec((1, d)), _const_spec((1, d))],
        out_specs=[
            qk_spec, qk_spec,
            pl.BlockSpec((None, DIFF_HEADS, blocks, VT_ROWS, t),
                         lambda bi, i: (bi, 0, i, 0, 0))],
        out_shape=[jax.ShapeDtypeStruct((b, s, d), _BF16),
                   jax.ShapeDtypeStruct((b, s, d), _BF16),
                   jax.ShapeDtypeStruct((b, DIFF_HEADS, s // t, VT_ROWS, t), _BF16)],
        compiler_params=_params(2),
        name="diff_qkv",
    )(x, mod, gain, w_qkv, ones_bd, gq_t, gk_t)


def _split3(x):
    hi = x.astype(_BF16).astype(_F32)
    r = x - hi
    mid = r.astype(_BF16).astype(_F32)
    lo = (r - mid).astype(_BF16).astype(_F32)
    return hi, mid, lo


def _flash_kernel(slopes_ref, q_ref, k_ref, vt_ref, lam_ref, subln_ref, o_ref,
                  kaug, dtile, acc_sc, *, lam_init):
    t = ATTN_TILE
    qb = QUERY_BLOCKS
    qt = qb * t
    n_blk = kaug.shape[1]
    hp = pl.program_id(1)
    qi = pl.program_id(2)
    lane = lax.broadcasted_iota(jnp.int32, (1, V_DIM), 1)

    @pl.when(qi == 0)
    def _build_tables():
        key = lax.broadcasted_iota(jnp.int32, (t, t), 0)
        qry = lax.broadcasted_iota(jnp.int32, (t, t), 1)
        ahead = jnp.maximum(key - qry, 0).astype(_F32)
        allowed = (key // CHUNK) <= (qry // CHUNK)
        key_pos = lax.broadcasted_iota(jnp.int32, (t, 1), 0).astype(_F32)
        for hh in range(HEADS_PER_STEP):
            sl = slopes_ref[hp * HEADS_PER_STEP + hh]
            hi, mid, lo = _split3(key_pos * sl)
            aug = jnp.where(lane == 0, hi, jnp.where(lane == 1, mid,
                            jnp.where(lane == 2, lo, 0.0))).astype(_BF16)
            for blk in range(n_blk):
                kaug[hh, blk, :, 0:V_DIM] = k_ref[blk * t:(blk + 1) * t,
                                                  hh * V_DIM:(hh + 1) * V_DIM]
                kaug[hh, blk, :, V_DIM:2 * V_DIM] = aug
            dtile[hh] = jnp.where(allowed, (2.0 * sl) * ahead, -_NEG_BIG)

    feat = lax.broadcasted_iota(jnp.int32, (V_DIM, 1), 0)
    bias_rows = jnp.broadcast_to(jnp.where(feat < 3, 1.0, 0.0), (V_DIM, 2 * qt))
    q2t = []
    for hh in range(HEADS_PER_STEP):
        qh_t = q_ref[:, hh * V_DIM:(hh + 1) * V_DIM].astype(_F32).T
        q_rows = jnp.concatenate([jnp.where(feat < DIFF_HEAD_DIM, qh_t, 0.0),
                                  jnp.where(feat < DIFF_HEAD_DIM, 0.0, qh_t)], axis=1)
        q2t.append(jnp.concatenate([q_rows, bias_rows], axis=0).astype(_BF16))
        acc_sc[hh] = jnp.zeros(acc_sc.shape[1:], _F32)

    def scores(hh, j):
        return _dot(kaug[hh, j], q2t[hh])

    def update(hh, j, s, m_prev, shift):
        m_next = jnp.maximum(m_prev, jnp.max(s, axis=0, keepdims=True) + shift)
        alpha = jnp.exp2(m_prev - m_next)
        p = jnp.exp2(s - (m_next - shift)).astype(_BF16)
        acc_sc[hh] = alpha * acc_sc[hh] + _dot(vt_ref[hh, j], p)
        return m_next

    def body(j, ms):
        rel_blocks = ((j - qb * qi) * t).astype(_F32)
        ss = [scores(hh, j) for hh in range(HEADS_PER_STEP)]
        return tuple(
            update(hh, j, ss[hh], ms[hh],
                   slopes_ref[hp * HEADS_PER_STEP + hh] * rel_blocks)
            for hh in range(HEADS_PER_STEP))

    def diag_scores(d, j):
        ss = [scores(hh, j) for hh in range(HEADS_PER_STEP)]
        out = []
        for hh in range(HEADS_PER_STEP):
            parts = []
            for slab in range(2 * qb):
                c = slab % qb
                blk = ss[hh][:, slab * t:(slab + 1) * t]
                if c < d:
                    parts.append(jnp.full((t, t), _NEG_BIG, _F32))
                elif c == d:
                    parts.append(blk - dtile[hh])
                else:
                    parts.append(blk)
            out.append(jnp.concatenate(parts, axis=1))
        return out

    def running_max_path():
        def diag_body(d):
            def fn(j, ms):
                ss = diag_scores(d, j)
                return tuple(
                    update(hh, j, ss[hh], ms[hh],
                           slopes_ref[hp * HEADS_PER_STEP + hh] * float(d * t))
                    for hh in range(HEADS_PER_STEP))
            return fn

        m0 = tuple(jnp.full((1, 2 * qt), _NEG_BIG, _F32) for _ in range(HEADS_PER_STEP))
        ms = lax.fori_loop(0, qb * qi, body, m0)
        for d in range(qb):
            ms = lax.fori_loop(qb * qi + d, qb * qi + d + 1, diag_body(d), ms)

    vec_rows = lam_ref[...]
    gq_max = jnp.max(jnp.abs(vec_rows[4:5, :]), axis=1, keepdims=True)
    gk_max = jnp.max(jnp.abs(vec_rows[5:6, :]), axis=1, keepdims=True)
    reach = (LOG2_E * gq_max) * (math.sqrt(DIFF_HEAD_DIM) * gk_max)
    bound_is_tight = 2.0 * reach[0, 0] < MAX_REFERENCE_GAP

    def fixed_reference_path():
        q_off = (lax.broadcasted_iota(jnp.int32, (1, 2 * qt), 1) % qt).astype(_F32)
        refs = [reach * BOUND_SLACK + BOUND_PAD
                + slopes_ref[hp * HEADS_PER_STEP + hh] * q_off
                for hh in range(HEADS_PER_STEP)]

        def accumulate(j, ss, shift_blocks):
            for hh in range(HEADS_PER_STEP):
                shift = slopes_ref[hp * HEADS_PER_STEP + hh] * shift_blocks
                p = jnp.exp2(ss[hh] - (refs[hh] - shift)).astype(_BF16)
                acc_sc[hh] = acc_sc[hh] + _dot(vt_ref[hh, j], p)

        def body_fixed(j, carry):
            accumulate(j, [scores(hh, j) for hh in range(HEADS_PER_STEP)],
                       ((j - qb * qi) * t).astype(_F32))
            return carry

        def diag_fixed(d):
            def fn(j, carry):
                accumulate(j, diag_scores(d, j), float(d * t))
                return carry
            return fn

        lax.fori_loop(0, qb * qi, body_fixed, 0)
        for d in range(qb):
            lax.fori_loop(qb * qi + d, qb * qi + d + 1, diag_fixed(d), 0)

    lax.cond(bound_is_tight, fixed_reference_path, running_max_path)

    lam_rows = lam_ref[...]
    dot1 = jnp.sum(lam_rows[0:1, :] * lam_rows[1:2, :], axis=-1, keepdims=True)
    dot2 = jnp.sum(lam_rows[2:3, :] * lam_rows[3:4, :], axis=-1, keepdims=True)
    lam = jnp.exp(dot1) - jnp.exp(dot2) + lam_init

    for hh in range(HEADS_PER_STEP):
        acc = acc_sc[hh]
        o_both = acc[0:V_DIM, :] * (1.0 / acc[V_DIM:V_DIM + 1, :])
        o = o_both[:, 0:qt] - lam * o_both[:, qt:2 * qt]
        o = o * lax.rsqrt(jnp.mean(o * o, axis=0, keepdims=True) + EPS)
        o = o * (subln_ref[...] * (1.0 - lam_init))
        o_ref[:, hh * V_DIM:(hh + 1) * V_DIM] = o.T.astype(o_ref.dtype)


def _flash_layer(q, k, vt, lam_rows, subln, lam_init):
    b, s, d = q.shape
    t = ATTN_TILE
    qt = QUERY_BLOCKS * t
    hps = HEADS_PER_STEP
    slopes = 2.0 ** (-8.0 * jnp.arange(1, DIFF_HEADS + 1, dtype=_F32) / DIFF_HEADS)
    grid_spec = pltpu.PrefetchScalarGridSpec(
        num_scalar_prefetch=1,
        grid=(b, DIFF_HEADS // hps, s // qt),
        in_specs=[
            pl.BlockSpec((None, qt, hps * V_DIM), lambda bi, h, i, sl: (bi, i, h)),
            pl.BlockSpec((None, s, hps * V_DIM), lambda bi, h, i, sl: (bi, 0, h),
                         pipeline_mode=pl.Buffered(1)),
            pl.BlockSpec((None, hps, s // t, VT_ROWS, t),
                         lambda bi, h, i, sl: (bi, h, 0, 0, 0),
                         pipeline_mode=pl.Buffered(1)),
            pl.BlockSpec(lam_rows.shape, lambda bi, h, i, sl: (0, 0)),
            pl.BlockSpec((V_DIM, 1), lambda bi, h, i, sl: (0, 0)),
        ],
        out_specs=pl.BlockSpec((None, qt, hps * V_DIM), lambda bi, h, i, sl: (bi, i, h)),
        scratch_shapes=[pltpu.VMEM((hps, s // t, t, 2 * V_DIM), _BF16),
                        pltpu.VMEM((hps, t, t), _F32),
                        pltpu.VMEM((hps, VT_ROWS, 2 * qt), _F32)],
    )
    return pl.pallas_call(
        functools.partial(_flash_kernel, lam_init=lam_init),
        grid_spec=grid_spec,
        out_shape=jax.ShapeDtypeStruct((b, s, d), _BF16),
        compiler_params=_params(3),
        name="diff_flash",
    )(slopes * LOG2_E, q, k, vt, lam_rows, subln)


def kernel(x, c, ada_w, ada_b, norm_mix, norm_ffn, ffn_w1, ffn_w3, ffn_w2, conv_w_in, conv_w, conv_w_out, diff_w_qkv, diff_q_norm, diff_k_norm, diff_lq1, diff_lk1, diff_lq2, diff_lk2, diff_subln, diff_w_out, pool_w_in, pool_w_group, pool_scale, pool_w_out):
    b = x.shape[0]
    bf = lambda w: w.astype(_BF16)
    mod_all = _ada_modulation(c, ada_w, ada_b).reshape(DEPTH, b, N_MOD, D_MODEL)
    for i in range(DEPTH):
        kind = i % N_MIXERS
        j = i // N_MIXERS
        mod = mod_all[i]
        gain = norm_mix[i].reshape(1, D_MODEL)
        pending = None
        if kind == 0:
            x = _conv_layer(x, mod, gain, bf(conv_w_in[j]), conv_w[j],
                            bf(conv_w_out[j]))
        elif kind == 1:
            lam_init = 0.8 - 0.6 * math.exp(-0.3 * i)
            q, k, vt = _qkv_layer(x, mod, gain, bf(diff_w_qkv[j]),
                                  diff_q_norm[j], diff_k_norm[j])
            lam_rows = jnp.stack([diff_lq1[j], diff_lk1[j], diff_lq2[j], diff_lk2[j],
                                  diff_q_norm[j], diff_k_norm[j]])
            a = _flash_layer(q, k, vt, lam_rows, diff_subln[j].reshape(V_DIM, 1),
                             lam_init)
            pending = (a, bf(diff_w_out[j]))
        else:
            x = _pool_layer(x, mod, gain, bf(pool_w_in[j]), bf(pool_w_group[j]),
                            pool_scale[j].reshape(1, D_MODEL), bf(pool_w_out[j]))
        x = _ffn_layer(x, mod, norm_ffn[i].reshape(1, D_MODEL), bf(ffn_w1[i]),
                       bf(ffn_w3[i]), bf(ffn_w2[i]), mixer=pending)
    return x
```

```python
import functools
import math

import jax
import jax.numpy as jnp
from jax import lax
from jax.experimental import pallas as pl
from jax.experimental.pallas import tpu as pltpu

D_MODEL = 1024
DEPTH = 4
CHUNK = 64
N_MIXERS = 3
D_FF = 2816
CONV_WIDTH = 3
DIFF_HEADS = 8
DIFF_HEAD_DIM = 64
V_DIM = 2 * DIFF_HEAD_DIM
POOL_WINDOWS = (2, 4, 8, 16)
POOL_GROUP_DIM = D_MODEL // len(POOL_WINDOWS)
N_MOD = 6
EPS = 1e-6

SUBLANES = 8
LANES = 128
VMEM_LIMIT_BYTES = 56 * 1024 * 1024

TOKEN_TILE = 1024
SUB_TILES = 4
ADA_TILE = 2048
MXU_TILE = 256
FFN_CHUNKS = 2
ATTN_TILE = 256
QUERY_BLOCKS = 2
HEADS_PER_STEP = 8
BF16_SUBLANES = 16
VT_ROWS = V_DIM + BF16_SUBLANES
LOG2_E = math.log2(math.e)
POOL_HALO = 16
CONV_HALO = 8

_BF16 = jnp.bfloat16
_F32 = jnp.float32
_NEG_BIG = -1e30
BOUND_SLACK = 1.0 + 2.0 ** -8
BOUND_PAD = 2.0 ** -4
MAX_REFERENCE_GAP = 100.0


def _dot(a, b):
    return jnp.dot(a, b, preferred_element_type=_F32)


def _const_spec(shape):
    n = len(shape)
    return pl.BlockSpec(shape, lambda *_: (0,) * n, pipeline_mode=pl.Buffered(1))


def _params(n_axes):
    return pltpu.CompilerParams(
        dimension_semantics=("arbitrary",) * n_axes,
        vmem_limit_bytes=VMEM_LIMIT_BYTES)


def _mod_norm(x, gain, scale, shift):
    ms = jnp.mean(x * x, axis=-1, keepdims=True)
    y = x * lax.rsqrt(ms + EPS) * gain
    return y * (1.0 + scale) + shift


def _ada_kernel(c_ref, w_ref, b_ref, o_ref):
    c = c_ref[...]
    cond = c * jax.nn.sigmoid(c)
    o_ref[...] = _dot(cond.astype(_BF16), w_ref[...].astype(_BF16)) + b_ref[...]


def _ada_modulation(c, ada_w, ada_b):
    b, d = c.shape
    n = ada_w.shape[-1]
    return pl.pallas_call(
        _ada_kernel,
        grid=(DEPTH, n // ADA_TILE),
        in_specs=[
            pl.BlockSpec((b, d), lambda l, j: (0, 0)),
            pl.BlockSpec((None, d, ADA_TILE), lambda l, j: (l, 0, j)),
            pl.BlockSpec((None, 1, ADA_TILE), lambda l, j: (l, 0, j)),
        ],
        out_specs=pl.BlockSpec((None, b, ADA_TILE), lambda l, j: (l, 0, j)),
        out_shape=jax.ShapeDtypeStruct((DEPTH, b, n), _F32),
        compiler_params=_params(2),
        name="ada_modulation",
    )(c, ada_w, ada_b.reshape(DEPTH, 1, n))


def _ffn_chunk_bounds():
    tiles = D_FF // MXU_TILE
    assert tiles * MXU_TILE == D_FF
    cuts = [(tiles * c + FFN_CHUNKS - 1) // FFN_CHUNKS * MXU_TILE
            for c in range(FFN_CHUNKS + 1)]
    return list(zip(cuts[:-1], cuts[1:]))


def _ffn_kernel(*refs, mixer_proj):
    if mixer_proj:
        x_ref, mod_ref, g_ref, a_ref, wo_ref, w1_ref, w3_ref, w2_ref, o_ref = refs
    else:
        x_ref, mod_ref, g_ref, w1_ref, w3_ref, w2_ref, o_ref = refs
    rows = TOKEN_TILE // SUB_TILES
    for s in range(SUB_TILES):
        rs = slice(s * rows, (s + 1) * rows)
        x = x_ref[rs, :]
        if mixer_proj:
            x = x + mod_ref[2:3, :] * _dot(a_ref[rs, :], wo_ref[...])
        h = _mod_norm(x, g_ref[...], mod_ref[4:5, :], mod_ref[3:4, :]).astype(_BF16)
        acc = None
        for lo, hi in _ffn_chunk_bounds():
            a = _dot(h, w1_ref[:, lo:hi])
            b = _dot(h, w3_ref[:, lo:hi])
            g = (a * jax.nn.sigmoid(a) * b).astype(_BF16)
            y = _dot(g, w2_ref[lo:hi, :])
            acc = y if acc is None else acc + y
        o_ref[rs, :] = x + mod_ref[5:6, :] * acc


def _token_spec():
    return pl.BlockSpec((None, TOKEN_TILE, D_MODEL), lambda b, i: (b, i, 0))


def _mod_spec():
    return pl.BlockSpec((None, N_MOD, D_MODEL), lambda b, i: (b, 0, 0))


def _ffn_layer(x, mod, gain, w1, w3, w2, mixer=None):
    b, s, d = x.shape
    mixer_specs = [] if mixer is None else [_token_spec(), _const_spec(mixer[1].shape)]
    return pl.pallas_call(
        functools.partial(_ffn_kernel, mixer_proj=mixer is not None),
        grid=(b, s // TOKEN_TILE),
        in_specs=[_token_spec(), _mod_spec(), _const_spec((1, d)), *mixer_specs,
                  _const_spec(w1.shape), _const_spec(w3.shape),
                  _const_spec(w2.shape)],
        out_specs=_token_spec(),
        out_shape=jax.ShapeDtypeStruct(x.shape, x.dtype),
        compiler_params=_params(2),
        name="ffn",
    )(x, mod, gain, *(mixer or ()), w1, w3, w2)


def _conv_kernel(x_ref, mod_ref, g_ref, win_ref, cw_ref, wout_ref, o_ref, ubuf):
    d = D_MODEL
    rows = TOKEN_TILE // SUB_TILES

    @pl.when(pl.program_id(1) == 0)
    def _():
        ubuf[0:CONV_HALO, :] = jnp.zeros((CONV_HALO, d), _F32)

    xs, bgs = [], []
    for s in range(SUB_TILES):
        x = x_ref[s * rows:(s + 1) * rows, :]
        h = _mod_norm(x, g_ref[...], mod_ref[1:2, :], mod_ref[0:1, :]).astype(_BF16)
        bgs.append(_dot(h, win_ref[:, 0:d]))
        cg = _dot(h, win_ref[:, d:2 * d])
        xv = _dot(h, win_ref[:, 2 * d:3 * d])
        ubuf[CONV_HALO + s * rows:CONV_HALO + (s + 1) * rows, :] = cg * xv
        xs.append(x)
    for s in range(SUB_TILES):
        r0 = CONV_HALO + s * rows
        y = (cw_ref[0:1, :] * ubuf[r0 - 2:r0 - 2 + rows, :]
             + cw_ref[1:2, :] * ubuf[r0 - 1:r0 - 1 + rows, :]
             + cw_ref[2:3, :] * ubuf[r0:r0 + rows, :])
        out = _dot((bgs[s] * y).astype(_BF16), wout_ref[...])
        o_ref[s * rows:(s + 1) * rows, :] = xs[s] + mod_ref[2:3, :] * out
    ubuf[0:CONV_HALO, :] = ubuf[TOKEN_TILE:TOKEN_TILE + CONV_HALO, :]


def _conv_layer(x, mod, gain, w_in, conv_w, w_out):
    b, s, d = x.shape
    return pl.pallas_call(
        _conv_kernel,
        grid=(b, s // TOKEN_TILE),
        in_specs=[_token_spec(), _mod_spec(), _const_spec((1, d)),
                  _const_spec(w_in.shape), _const_spec(conv_w.shape),
                  _const_spec(w_out.shape)],
        out_specs=_token_spec(),
        out_shape=jax.ShapeDtypeStruct(x.shape, x.dtype),
        scratch_shapes=[pltpu.VMEM((TOKEN_TILE + CONV_HALO, d), _F32)],
        compiler_params=_params(2),
        name="conv_mixer",
    )(x, mod, gain, w_in, conv_w, w_out)


def _pool_kernel(x_ref, mod_ref, g_ref, win_ref, wg_ref, sc_ref, wout_ref,
                 o_ref, *bufs):
    gd = POOL_GROUP_DIM
    n_grp = len(POOL_WINDOWS)
    rows = TOKEN_TILE // SUB_TILES
    i = pl.program_id(1)

    @pl.when(i == 0)
    def _():
        for buf in bufs:
            buf[0:POOL_HALO, :] = jnp.zeros((POOL_HALO, buf.shape[1]), _F32)

    xs = []
    for s in range(SUB_TILES):
        x = x_ref[s * rows:(s + 1) * rows, :]
        h = _mod_norm(x, g_ref[...], mod_ref[1:2, :], mod_ref[0:1, :]).astype(_BF16)
        bufs[0][POOL_HALO + s * rows:POOL_HALO + (s + 1) * rows, :] = _dot(h, win_ref[...])
        xs.append(x)
    for s in range(SUB_TILES):
        r0 = POOL_HALO + s * rows
        pos = i * TOKEN_TILE + s * rows + lax.broadcasted_iota(jnp.int32, (rows, 1), 0)
        u = bufs[0][r0:r0 + rows, :]
        cur = u
        ys = []
        for k in range(n_grp):
            w = 2 ** k
            cur = cur + bufs[k][r0 - w:r0 - w + rows, :]
            cnt = jnp.minimum(pos + 1, 2 * w).astype(_F32)
            pooled = cur[:, 0:gd] / cnt - u[:, k * gd:(k + 1) * gd]
            ys.append(_dot(pooled.astype(_BF16), wg_ref[k]))
            if k + 1 < n_grp:
                cur = cur[:, gd:]
                bufs[k + 1][r0:r0 + rows, :] = cur
        y = jnp.concatenate(ys, axis=-1) * sc_ref[...]
        out = _dot(y.astype(_BF16), wout_ref[...])
        o_ref[s * rows:(s + 1) * rows, :] = xs[s] + mod_ref[2:3, :] * out
    for buf in bufs:
        buf[0:POOL_HALO, :] = buf[TOKEN_TILE:TOKEN_TILE + POOL_HALO, :]


def _pool_layer(x, mod, gain, w_in, w_group, scale, w_out):
    b, s, d = x.shape
    n_grp = len(POOL_WINDOWS)
    assert POOL_WINDOWS == tuple(2 ** (g + 1) for g in range(n_grp))
    assert POOL_HALO >= POOL_WINDOWS[-1] // 2
    return pl.pallas_call(
        _pool_kernel,
        grid=(b, s // TOKEN_TILE),
        in_specs=[_token_spec(), _mod_spec(), _const_spec((1, d)),
                  _const_spec(w_in.shape), _const_spec(w_group.shape),
                  _const_spec((1, d)), _const_spec(w_out.shape)],
        out_specs=_token_spec(),
        out_shape=jax.ShapeDtypeStruct(x.shape, x.dtype),
        scratch_shapes=[
            pltpu.VMEM((TOKEN_TILE + POOL_HALO, (n_grp - k) * POOL_GROUP_DIM), _F32)
            for k in range(n_grp)],
        compiler_params=_params(2),
        name="pool_mixer",
    )(x, mod, gain, w_in, w_group, scale, w_out)


def _group_rms_scale(t, ones_ref):
    sq = t * t
    hi = sq.astype(_BF16)
    lo = (sq - hi.astype(_F32)).astype(_BF16)
    sums = []
    for c in range(t.shape[1] // MXU_TILE):
        cols = slice(c * MXU_TILE, (c + 1) * MXU_TILE)
        sums.append(_dot(jnp.concatenate([hi[:, cols], lo[:, cols]], axis=1),
                         ones_ref[...]))
    ss = jnp.concatenate(sums, axis=1)
    return lax.rsqrt(ss * (1.0 / DIFF_HEAD_DIM) + EPS)


def _qkv_kernel(x_ref, mod_ref, g_ref, w_ref, ones_ref, gq_ref, gk_ref,
                q_ref, k_ref, vt_ref):
    d = D_MODEL
    x = x_ref[...]
    h = _mod_norm(x, g_ref[...], mod_ref[1:2, :], mod_ref[0:1, :]).astype(_BF16)
    q = _dot(h, w_ref[:, 0:d])
    k = _dot(h, w_ref[:, d:2 * d])
    v = _dot(h, w_ref[:, 2 * d:3 * d])
    q_scale = DIFF_HEAD_DIM ** -0.5 * LOG2_E
    qn = q * _group_rms_scale(q, ones_ref) * gq_ref[...] * q_scale
    kn = k * _group_rms_scale(k, ones_ref) * gk_ref[...]
    q_ref[...] = qn.astype(_BF16)
    k_ref[...] = kn.astype(_BF16)
    t = ATTN_TILE
    pad_rows = VT_ROWS - V_DIM
    ones_row = (lax.broadcasted_iota(jnp.int32, (pad_rows, t), 0) == 0)
    ones_row = ones_row.astype(_F32).astype(_BF16)
    for hd in range(DIFF_HEADS):
        for blk in range(TOKEN_TILE // t):
            vb = v[blk * t:(blk + 1) * t, hd * V_DIM:(hd + 1) * V_DIM]
            vt_ref[hd, blk, 0:V_DIM, :] = vb.T.astype(_BF16)
            vt_ref[hd, blk, V_DIM:VT_ROWS, :] = ones_row


def _qkv_layer(x, mod, gain, w_qkv, gq, gk):
    b, s, d = x.shape
    t = ATTN_TILE
    groups = d // DIFF_HEAD_DIM
    col_group = jnp.arange(MXU_TILE, dtype=jnp.int32) // DIFF_HEAD_DIM
    ones_bd = (col_group[:, None] == col_group[None, :]).astype(_BF16)
    ones_bd = jnp.concatenate([ones_bd, ones_bd], axis=0)
    gq_t = jnp.tile(gq, groups).reshape(1, d)
    gk_t = jnp.tile(gk, groups).reshape(1, d)
    qk_spec = pl.BlockSpec((None, TOKEN_TILE, d), lambda bi, i: (bi, i, 0))
    blocks = TOKEN_TILE // t
    return pl.pallas_call(
        _qkv_kernel,
        grid=(b, s // TOKEN_TILE),
        in_specs=[_token_spec(), _mod_spec(), _const_spec((1, d)),
                  _const_spec(w_qkv.shape), _const_spec(ones_bd.shape),
                  _const_spec((1, d)), _const_spec((1, d))],
        out_specs=[
            qk_spec, qk_spec,
            pl.BlockSpec((None, DIFF_HEADS, blocks, VT_ROWS, t),
                         lambda bi, i: (bi, 0, i, 0, 0))],
        out_shape=[jax.ShapeDtypeStruct((b, s, d), _BF16),
                   jax.ShapeDtypeStruct((b, s, d), _BF16),
                   jax.ShapeDtypeStruct((b, DIFF_HEADS, s // t, VT_ROWS, t), _BF16)],
        compiler_params=_params(2),
        name="diff_qkv",
    )(x, mod, gain, w_qkv, ones_bd, gq_t, gk_t)


def _split3(x):
    hi = x.astype(_BF16).astype(_F32)
    r = x - hi
    mid = r.astype(_BF16).astype(_F32)
    lo = (r - mid).astype(_BF16).astype(_F32)
    return hi, mid, lo


def _flash_kernel(slopes_ref, q_ref, k_ref, vt_ref, lam_ref, subln_ref, o_ref,
                  kaug, dtile, acc_sc, *, lam_init):
    t = ATTN_TILE
    qb = QUERY_BLOCKS
    qt = qb * t
    n_blk = kaug.shape[1]
    hp = pl.program_id(1)
    qi = pl.program_id(2)
    lane = lax.broadcasted_iota(jnp.int32, (1, V_DIM), 1)

    @pl.when(qi == 0)
    def _build_tables():
        key = lax.broadcasted_iota(jnp.int32, (t, t), 0)
        qry = lax.broadcasted_iota(jnp.int32, (t, t), 1)
        ahead = jnp.maximum(key - qry, 0).astype(_F32)
        allowed = (key // CHUNK) <= (qry // CHUNK)
        key_pos = lax.broadcasted_iota(jnp.int32, (t, 1), 0).astype(_F32)
        for hh in range(HEADS_PER_STEP):
            sl = slopes_ref[hp * HEADS_PER_STEP + hh]
            hi, mid, lo = _split3(key_pos * sl)
            aug = jnp.where(lane == 0, hi, jnp.where(lane == 1, mid,
                            jnp.where(lane == 2, lo, 0.0))).astype(_BF16)
            for blk in range(n_blk):
                kaug[hh, blk, :, 0:V_DIM] = k_ref[blk * t:(blk + 1) * t,
                                                  hh * V_DIM:(hh + 1) * V_DIM]
                kaug[hh, blk, :, V_DIM:2 * V_DIM] = aug
            dtile[hh] = jnp.where(allowed, (2.0 * sl) * ahead, -_NEG_BIG)

    feat = lax.broadcasted_iota(jnp.int32, (V_DIM, 1), 0)
    bias_rows = jnp.broadcast_to(jnp.where(feat < 3, 1.0, 0.0), (V_DIM, 2 * qt))
    q2t = []
    for hh in range(HEADS_PER_STEP):
        qh_t = q_ref[:, hh * V_DIM:(hh + 1) * V_DIM].astype(_F32).T
        q_rows = jnp.concatenate([jnp.where(feat < DIFF_HEAD_DIM, qh_t, 0.0),
                                  jnp.where(feat < DIFF_HEAD_DIM, 0.0, qh_t)], axis=1)
        q2t.append(jnp.concatenate([q_rows, bias_rows], axis=0).astype(_BF16))
        acc_sc[hh] = jnp.zeros(acc_sc.shape[1:], _F32)

    def scores(hh, j):
        return _dot(kaug[hh, j], q2t[hh])

    def update(hh, j, s, m_prev, shift):
        m_next = jnp.maximum(m_prev, jnp.max(s, axis=0, keepdims=True) + shift)
        alpha = jnp.exp2(m_prev - m_next)
        p = jnp.exp2(s - (m_next - shift)).astype(_BF16)
        acc_sc[hh] = alpha * acc_sc[hh] + _dot(vt_ref[hh, j], p)
        return m_next

    def body(j, ms):
        rel_blocks = ((j - qb * qi) * t).astype(_F32)
        ss = [scores(hh, j) for hh in range(HEADS_PER_STEP)]
        return tuple(
            update(hh, j, ss[hh], ms[hh],
                   slopes_ref[hp * HEADS_PER_STEP + hh] * rel_blocks)
            for hh in range(HEADS_PER_STEP))

    def diag_scores(d, j):
        ss = [scores(hh, j) for hh in range(HEADS_PER_STEP)]
        out = []
        for hh in range(HEADS_PER_STEP):
            parts = []
            for slab in range(2 * qb):
                c = slab % qb
                blk = ss[hh][:, slab * t:(slab + 1) * t]
                if c < d:
                    parts.append(jnp.full((t, t), _NEG_BIG, _F32))
                elif c == d:
                    parts.append(blk - dtile[hh])
                else:
                    parts.append(blk)
            out.append(jnp.concatenate(parts, axis=1))
        return out

    def running_max_path():
        def diag_body(d):
            def fn(j, ms):
                ss = diag_scores(d, j)
                return tuple(
                    update(hh, j, ss[hh], ms[hh],
                           slopes_ref[hp * HEADS_PER_STEP + hh] * float(d * t))
                    for hh in range(HEADS_PER_STEP))
            return fn

        m0 = tuple(jnp.full((1, 2 * qt), _NEG_BIG, _F32) for _ in range(HEADS_PER_STEP))
        ms = lax.fori_loop(0, qb * qi, body, m0)
        for d in range(qb):
            ms = lax.fori_loop(qb * qi + d, qb * qi + d + 1, diag_body(d), ms)

    vec_rows = lam_ref[...]
    gq_max = jnp.max(jnp.abs(vec_rows[4:5, :]), axis=1, keepdims=True)
    gk_max = jnp.max(jnp.abs(vec_rows[5:6, :]), axis=1, keepdims=True)
    reach = (LOG2_E * gq_max) * (math.sqrt(DIFF_HEAD_DIM) * gk_max)
    bound_is_tight = 2.0 * reach[0, 0] < MAX_REFERENCE_GAP

    def fixed_reference_path():
        q_off = (lax.broadcasted_iota(jnp.int32, (1, 2 * qt), 1) % qt).astype(_F32)
        refs = [reach * BOUND_SLACK + BOUND_PAD
                + slopes_ref[hp * HEADS_PER_STEP + hh] * q_off
                for hh in range(HEADS_PER_STEP)]

        def accumulate(j, ss, shift_blocks):
            for hh in range(HEADS_PER_STEP):
                shift = slopes_ref[hp * HEADS_PER_STEP + hh] * shift_blocks
                p = jnp.exp2(ss[hh] - (refs[hh] - shift)).astype(_BF16)
                acc_sc[hh] = acc_sc[hh] + _dot(vt_ref[hh, j], p)

        def body_fixed(jj, carry):
            ss = [scores(hh, qb * jj) for hh in range(HEADS_PER_STEP)]
            for r in range(qb):
                j = qb * jj + r
                shift_blocks = ((j - qb * qi) * t).astype(_F32)
                ss_next = []
                for hh in range(HEADS_PER_STEP):
                    if r + 1 < qb:
                        ss_next.append(scores(hh, j + 1))
                    shift = slopes_ref[hp * HEADS_PER_STEP + hh] * shift_blocks
                    p = jnp.exp2(ss[hh] - (refs[hh] - shift)).astype(_BF16)
                    acc_sc[hh] = acc_sc[hh] + _dot(vt_ref[hh, j], p)
                ss = ss_next
            return carry

        def diag_fixed(d):
            def fn(j, carry):
                accumulate(j, diag_scores(d, j), float(d * t))
                return carry
            return fn

        lax.fori_loop(0, qi, body_fixed, 0)
        for d in range(qb):
            lax.fori_loop(qb * qi + d, qb * qi + d + 1, diag_fixed(d), 0)

    lax.cond(bound_is_tight, fixed_reference_path, running_max_path)

    lam_rows = lam_ref[...]
    dot1 = jnp.sum(lam_rows[0:1, :] * lam_rows[1:2, :], axis=-1, keepdims=True)
    dot2 = jnp.sum(lam_rows[2:3, :] * lam_rows[3:4, :], axis=-1, keepdims=True)
    lam = jnp.exp(dot1) - jnp.exp(dot2) + lam_init

    for hh in range(HEADS_PER_STEP):
        acc = acc_sc[hh]
        o_both = acc[0:V_DIM, :] * (1.0 / acc[V_DIM:V_DIM + 1, :])
        o = o_both[:, 0:qt] - lam * o_both[:, qt:2 * qt]
        o = o * lax.rsqrt(jnp.mean(o * o, axis=0, keepdims=True) + EPS)
        o = o * (subln_ref[...] * (1.0 - lam_init))
        o_ref[:, hh * V_DIM:(hh + 1) * V_DIM] = o.T.astype(o_ref.dtype)


def _flash_layer(q, k, vt, lam_rows, subln, lam_init):
    b, s, d = q.shape
    t = ATTN_TILE
    qt = QUERY_BLOCKS * t
    hps = HEADS_PER_STEP
    slopes = 2.0 ** (-8.0 * jnp.arange(1, DIFF_HEADS + 1, dtype=_F32) / DIFF_HEADS)
    grid_spec = pltpu.PrefetchScalarGridSpec(
        num_scalar_prefetch=1,
        grid=(b, DIFF_HEADS // hps, s // qt),
        in_specs=[
            pl.BlockSpec((None, qt, hps * V_DIM), lambda bi, h, i, sl: (bi, i, h)),
            pl.BlockSpec((None, s, hps * V_DIM), lambda bi, h, i, sl: (bi, 0, h),
                         pipeline_mode=pl.Buffered(1)),
            pl.BlockSpec((None, hps, s // t, VT_ROWS, t),
                         lambda bi, h, i, sl: (bi, h, 0, 0, 0),
                         pipeline_mode=pl.Buffered(1)),
            pl.BlockSpec(lam_rows.shape, lambda bi, h, i, sl: (0, 0)),
            pl.BlockSpec((V_DIM, 1), lambda bi, h, i, sl: (0, 0)),
        ],
        out_specs=pl.BlockSpec((None, qt, hps * V_DIM), lambda bi, h, i, sl: (bi, i, h)),
        scratch_shapes=[pltpu.VMEM((hps, s // t, t, 2 * V_DIM), _BF16),
                        pltpu.VMEM((hps, t, t), _F32),
                        pltpu.VMEM((hps, VT_ROWS, 2 * qt), _F32)],
    )
    return pl.pallas_call(
        functools.partial(_flash_kernel, lam_init=lam_init),
        grid_spec=grid_spec,
        out_shape=jax.ShapeDtypeStruct((b, s, d), _BF16),
        compiler_params=_params(3),
        name="diff_flash",
    )(slopes * LOG2_E, q, k, vt, lam_rows, subln)


def kernel(x, c, ada_w, ada_b, norm_mix, norm_ffn, ffn_w1, ffn_w3, ffn_w2, conv_w_in, conv_w, conv_w_out, diff_w_qkv, diff_q_norm, diff_k_norm, diff_lq1, diff_lk1, diff_lq2, diff_lk2, diff_subln, diff_w_out, pool_w_in, pool_w_group, pool_scale, pool_w_out):
    b = x.shape[0]
    bf = lambda w: w.astype(_BF16)
    mod_all = _ada_modulation(c, ada_w, ada_b).reshape(DEPTH, b, N_MOD, D_MODEL)
    for i in range(DEPTH):
        kind = i % N_MIXERS
        j = i // N_MIXERS
        mod = mod_all[i]
        gain = norm_mix[i].reshape(1, D_MODEL)
        pending = None
        if kind == 0:
            x = _conv_layer(x, mod, gain, bf(conv_w_in[j]), conv_w[j],
                            bf(conv_w_out[j]))
        elif kind == 1:
            lam_init = 0.8 - 0.6 * math.exp(-0.3 * i)
            q, k, vt = _qkv_layer(x, mod, gain, bf(diff_w_qkv[j]),
                                  diff_q_norm[j], diff_k_norm[j])
            lam_rows = jnp.stack([diff_lq1[j], diff_lk1[j], diff_lq2[j], diff_lk2[j],
                                  diff_q_norm[j], diff_k_norm[j]])
            a = _flash_layer(q, k, vt, lam_rows, diff_subln[j].reshape(V_DIM, 1),
                             lam_init)
            pending = (a, bf(diff_w_out[j]))
        else:
            x = _pool_layer(x, mod, gain, bf(pool_w_in[j]), bf(pool_w_group[j]),
                            pool_scale[j].reshape(1, D_MODEL), bf(pool_w_out[j]))
        x = _ffn_layer(x, mod, norm_ffn[i].reshape(1, D_MODEL), bf(ffn_w1[i]),
                       bf(ffn_w3[i]), bf(ffn_w2[i]), mixer=pending)
    return x
```
